```python
import math
import jax, jax.numpy as jnp
from jax import lax
import numpy as np

D_MODEL = 1024
BATCH = 8
SEQ = 8192
DEPTH = 1
DEC_BATCH = 4
DEC_SEQ = 8192
PAST_LEN = 128

N_RET_HEADS = 8
RET_QK_DIM = D_MODEL // 16
RET_V_DIM = D_MODEL // N_RET_HEADS
RET_QK = N_RET_HEADS * RET_QK_DIM
RET_V = N_RET_HEADS * RET_V_DIM
CHUNK = 128
ROPE_BASE = 10000.0
POOL_WINDOWS = (2, 4, 8, 16)
N_POOL_GROUPS = 4
POOL_WIDTH = D_MODEL // 2
POOL_GROUP_IN = POOL_WIDTH // N_POOL_GROUPS
POOL_GROUP_OUT = D_MODEL // N_POOL_GROUPS
D_FF = 128 * ((8 * D_MODEL // 3 + 127) // 128)
CONV_WIDTH = 3
N_MOD = 6
EPS = 1e-6
IN_WIDTH = 2 * RET_QK + 2 * RET_V + POOL_WIDTH + 2 * D_MODEL
IN_SPLITS = (RET_QK, 2 * RET_QK, 2 * RET_QK + RET_V, 2 * RET_QK + 2 * RET_V,
             2 * RET_QK + 2 * RET_V + POOL_WIDTH, 2 * RET_QK + 2 * RET_V + POOL_WIDTH + D_MODEL)

kernel_name = "gated_retention_pool_encoder"


def _rmsnorm(x, g):
    xf = x.astype(jnp.float32)
    return xf * lax.rsqrt(jnp.mean(xf * xf, axis=-1, keepdims=True) + EPS) * g.astype(jnp.float32)


def _rope_tables(s):
    half = RET_QK_DIM // 2
    inv_freq = ROPE_BASE ** (-jnp.arange(half, dtype=jnp.float32) / half)
    ang = jnp.arange(s, dtype=jnp.float32)[:, None] * inv_freq[None, :]
    return jnp.cos(ang)[None, :, None, :], jnp.sin(ang)[None, :, None, :]


def _rotary(t, cos, sin):
    t1, t2 = jnp.split(t, 2, axis=-1)
    return jnp.concatenate([t1 * cos - t2 * sin, t1 * sin + t2 * cos], axis=-1)


def _retention_one_direction(q, k, v, log_g, strict):
    b, s, h, dk = q.shape
    dv = v.shape[-1]
    n = s // CHUNK

    def chunks(t):
        return t.reshape(b, n, CHUNK, h, t.shape[-1]).transpose(1, 0, 3, 2, 4)

    idx = jnp.arange(CHUNK, dtype=jnp.float32)
    diff = idx[:, None] - idx[None, :]
    allowed = diff >= (1.0 if strict else 0.0)
    decay_in = jnp.where(allowed[None], jnp.exp(jnp.where(allowed, diff, 0.0)[None] * log_g[:, None, None]), 0.0)
    decay_q = jnp.exp((idx + 1.0)[None, :] * log_g[:, None])[None, :, :, None]
    decay_k = jnp.exp((CHUNK - 1.0 - idx)[None, :] * log_g[:, None])[None, :, :, None]
    decay_chunk = jnp.exp(CHUNK * log_g)[None, :, None, None]

    def step(state, qkv):
        qc, kc, vc = qkv
        scores = jnp.einsum('bhqd,bhkd->bhqk', qc, kc) * decay_in
        inner = jnp.einsum('bhqk,bhke->bhqe', scores, vc)
        cross = jnp.einsum('bhqd,bhde->bhqe', qc * decay_q, state)
        new_state = decay_chunk * state + jnp.einsum('bhkd,bhke->bhde', kc * decay_k, vc)
        return new_state, inner + cross

    state0 = jnp.zeros((b, h, dk, dv), jnp.float32)
    _, out = lax.scan(step, state0, (chunks(q), chunks(k), chunks(v)))
    return out.transpose(1, 0, 3, 2, 4).reshape(b, s, h, dv)


def _multiscale_pool(p):
    s = p.shape[1]
    pos = jnp.arange(s)
    outs = []
    for gi, w in enumerate(POOL_WINDOWS):
        pg = p[..., gi * POOL_GROUP_IN:(gi + 1) * POOL_GROUP_IN]
        cs = jnp.pad(jnp.cumsum(pg, axis=1), ((0, 0), (1, 0), (0, 0)))
        lo = jnp.clip(pos - w // 2, 0, s)
        hi = jnp.clip(pos + w // 2, 0, s)
        cnt = (hi - lo).astype(jnp.float32)
        mean = (jnp.take(cs, hi, axis=1) - jnp.take(cs, lo, axis=1)) / cnt[None, :, None]
        outs.append(mean - pg)
    return jnp.stack(outs, axis=2)


def _dwconv_centred(u, w, bias):
    up = jnp.pad(u, ((0, 0), (1, 1), (0, 0)))
    return up[:, :-2] * w[0] + up[:, 1:-1] * w[1] + up[:, 2:] * w[2] + bias


def _layer(x, c, w_ada, b_ada, g_mix, w_in, ret_decay_f, ret_decay_b, ret_gn_gain,
           w_pool_grp, pool_scale, w_out, g_ffn, w_up, conv_w, conv_b, w_down):
    dt = x.dtype
    b, s, _ = x.shape
    mod = jnp.dot(jax.nn.silu(c), w_ada) + b_ada
    shift1, scale1, gate1, shift2, scale2, gate2 = [m[:, None, :] for m in jnp.split(mod, N_MOD, axis=-1)]

    h = (_rmsnorm(x, g_mix) * (1.0 + scale1) + shift1).astype(dt)
    proj = jnp.einsum('bsd,de->bse', h, w_in)
    q, k, v, g_sw, p, gate_a, gate_b = jnp.split(proj, IN_SPLITS, axis=-1)

    cos, sin = _rope_tables(s)
    qh = _rotary(q.astype(jnp.float32).reshape(b, s, N_RET_HEADS, RET_QK_DIM), cos, sin)
    kh = _rotary(k.astype(jnp.float32).reshape(b, s, N_RET_HEADS, RET_QK_DIM), cos, sin) * (RET_QK_DIM ** -0.5)
    vh = v.astype(jnp.float32).reshape(b, s, N_RET_HEADS, RET_V_DIM)
    log_g_f = jax.nn.log_sigmoid(ret_decay_f.astype(jnp.float32))
    log_g_b = jax.nn.log_sigmoid(ret_decay_b.astype(jnp.float32))
    ret_f = _retention_one_direction(qh, kh, vh, log_g_f, strict=False)
    ret_b = _retention_one_direction(qh[:, ::-1], kh[:, ::-1], vh[:, ::-1], log_g_b, strict=True)[:, ::-1]
    ret = ret_f + ret_b
    mu = jnp.mean(ret, axis=-1, keepdims=True)
    var = jnp.mean(jnp.square(ret - mu), axis=-1, keepdims=True)
    ret = ((ret - mu) * lax.rsqrt(var + EPS)).reshape(b, s, RET_V) * ret_gn_gain.astype(jnp.float32)
    branch_a = jax.nn.silu(g_sw.astype(jnp.float32)) * ret

    pooled = _multiscale_pool(p.astype(jnp.float32))
    branch_b = jnp.einsum('bsgi,gio->bsgo', pooled, w_pool_grp.astype(jnp.float32)).reshape(b, s, D_MODEL)
    branch_b = branch_b * pool_scale.astype(jnp.float32)

    merged = jax.nn.sigmoid(gate_a.astype(jnp.float32)) * branch_a + jax.nn.sigmoid(gate_b.astype(jnp.float32)) * branch_b
    mix_out = jnp.einsum('bsd,de->bse', merged.astype(dt), w_out)
    x = x + gate1 * mix_out

    h2 = (_rmsnorm(x, g_ffn) * (1.0 + scale2) + shift2).astype(dt)
    up = jnp.einsum('bsd,df->bsf', h2, w_up)
    up = _dwconv_centred(up, conv_w, conv_b)
    act, lin = jnp.split(up, 2, axis=-1)
    ffn = jnp.einsum('bsf,fd->bsd', jax.nn.gelu(act) * lin, w_down)
    return x + gate2 * ffn


def _encoder(x, c, w_ada, b_ada, g_mix, w_in, ret_decay_f, ret_decay_b, ret_gn_gain,
             w_pool_grp, pool_scale, w_out, g_ffn, w_up, conv_w, conv_b, w_down, g_final):
    for l in range(DEPTH):
        x = _layer(x, c, w_ada[l], b_ada[l], g_mix[l], w_in[l], ret_decay_f[l], ret_decay_b[l],
                   ret_gn_gain[l], w_pool_grp[l], pool_scale[l], w_out[l], g_ffn[l], w_up[l],
                   conv_w[l], conv_b[l], w_down[l])
    return _rmsnorm(x, g_final).astype(x.dtype)


def setup_inputs(seed: int = 0) -> dict:
    key = jax.random.key(seed)
    ks = jax.random.split(key, 24)
    f32 = jnp.float32
    nrm = lambda k, shape, scale: jax.random.normal(k, shape, f32) * scale
    base_logit = jnp.log(2.0 ** (5.0 + jnp.arange(N_RET_HEADS, dtype=f32)) - 1.0)
    return {
        "x_prompt": nrm(ks[0], (BATCH, SEQ, D_MODEL), 1.0),
        "x_sample": nrm(ks[1], (DEC_BATCH, DEC_SEQ, D_MODEL), 1.0),
        "c_prompt": nrm(ks[2], (BATCH, D_MODEL), 1.0),
        "c_sample": nrm(ks[3], (DEC_BATCH, D_MODEL), 1.0),
        "w_ada": nrm(ks[4], (DEPTH, D_MODEL, N_MOD * D_MODEL), D_MODEL ** -0.5),
        "b_ada": nrm(ks[5], (DEPTH, N_MOD * D_MODEL), 0.02),
        "g_mix": 1.0 + nrm(ks[6], (DEPTH, D_MODEL), 0.05),
        "w_in": nrm(ks[7], (DEPTH, D_MODEL, IN_WIDTH), D_MODEL ** -0.5),
        "ret_decay_f": base_logit[None, :] + nrm(ks[8], (DEPTH, N_RET_HEADS), 0.1),
        "ret_decay_b": base_logit[None, :] + nrm(ks[9], (DEPTH, N_RET_HEADS), 0.1),
        "ret_gn_gain": 1.0 + nrm(ks[10], (DEPTH, RET_V), 0.05),
        "w_pool_grp": nrm(ks[11], (DEPTH, N_POOL_GROUPS, POOL_GROUP_IN, POOL_GROUP_OUT), POOL_GROUP_IN ** -0.5),
        "pool_scale": 1.0 + nrm(ks[12], (DEPTH, D_MODEL), 0.1),
        "w_out": nrm(ks[13], (DEPTH, D_MODEL, D_MODEL), D_MODEL ** -0.5),
        "g_ffn": 1.0 + nrm(ks[14], (DEPTH, D_MODEL), 0.05),
        "w_up": nrm(ks[15], (DEPTH, D_MODEL, 2 * D_FF), D_MODEL ** -0.5),
        "conv_w": nrm(ks[16], (DEPTH, CONV_WIDTH, 2 * D_FF), CONV_WIDTH ** -0.5),
        "conv_b": nrm(ks[17], (DEPTH, 2 * D_FF), 0.02),
        "w_down": nrm(ks[18], (DEPTH, D_FF, D_MODEL), D_FF ** -0.5),
        "g_final": 1.0 + nrm(ks[19], (D_MODEL,), 0.05),
    }


def reference(x_prompt, x_sample, c_prompt, c_sample, w_ada, b_ada, g_mix, w_in, ret_decay_f,
              ret_decay_b, ret_gn_gain, w_pool_grp, pool_scale, w_out, g_ffn, w_up, conv_w, conv_b,
              w_down, g_final):
    y_prompt = _encoder(x_prompt, c_prompt, w_ada, b_ada, g_mix, w_in, ret_decay_f, ret_decay_b,
                        ret_gn_gain, w_pool_grp, pool_scale, w_out, g_ffn, w_up, conv_w, conv_b,
                        w_down, g_final)
    y_sample = _encoder(x_sample, c_sample, w_ada, b_ada, g_mix, w_in, ret_decay_f, ret_decay_b,
                        ret_gn_gain, w_pool_grp, pool_scale, w_out, g_ffn, w_up, conv_w, conv_b,
                        w_down, g_final)
    return (y_prompt, y_sample)
```

```python
import functools
import math

import jax
import jax.numpy as jnp
from jax import lax
from jax.experimental import pallas as pl
from jax.experimental.pallas import tpu as pltpu

D_MODEL = 1024
N_HEADS = 8
QK_DIM = 64
V_DIM = 128
RET_QK = N_HEADS * QK_DIM
RET_V = N_HEADS * V_DIM
N_PAIRS = N_HEADS // 2
CHUNK = 128
ROPE_BASE = 10000.0
POOL_WINDOWS = (2, 4, 8, 16)
N_POOL_GROUPS = 4
POOL_WIDTH = 512
POOL_GROUP_IN = 128
POOL_GROUP_OUT = 256
D_FF = 2816
N_MOD = 6
EPS = 1e-6
IN_WIDTH = 2 * RET_QK + 2 * RET_V + POOL_WIDTH + 2 * D_MODEL
OFF_Q, OFF_K, OFF_V, OFF_GSW, OFF_P, OFF_GA, OFF_GB = 0, 512, 1024, 2048, 3072, 3584, 4608

HALO = 8
BF16_ROWS = 16
FF_BLOCK = 256
VMEM_LIMIT_BYTES = 56 * 1024 * 1024

_F32 = jnp.float32
_BF16 = jnp.bfloat16


def _const_spec(shape):
    nd = len(shape)
    return pl.BlockSpec(shape, lambda *_: (0,) * nd, pipeline_mode=pl.Buffered(1))


def _sigmoid(x):
    return 1.0 / (1.0 + jnp.exp(-x))


def _rms_scale(x):
    return lax.rsqrt(jnp.mean(x * x, axis=-1, keepdims=True) + EPS)


def _ada_kernel(c_ref, w_ref, b_ref, o_ref):
    c = c_ref[...]
    a = (c * _sigmoid(c)).astype(_BF16)
    o_ref[...] = jnp.dot(a, w_ref[...], preferred_element_type=_F32) + b_ref[...]


def _ada_call(c, w_ada_b, b_ada):
    b = c.shape[0]
    n = w_ada_b.shape[1]
    nb = D_MODEL
    return pl.pallas_call(
        _ada_kernel,
        grid=(n // nb,),
        in_specs=[pl.BlockSpec((b, D_MODEL), lambda j: (0, 0)),
                  pl.BlockSpec((D_MODEL, nb), lambda j: (0, j)),
                  pl.BlockSpec((1, nb), lambda j: (0, j))],
        out_specs=pl.BlockSpec((b, nb), lambda j: (0, j)),
        out_shape=jax.ShapeDtypeStruct((b, n), _F32),
        name="ada",
    )(c, w_ada_b, b_ada.reshape(1, n))


def _rotary(t, cos, sin_signed, first_half):
    partner = jnp.where(first_half, pltpu.roll(t, 96, axis=1), pltpu.roll(t, 32, axis=1))
    return t * cos + partner * sin_signed


def _proj_kernel(x_ref, mod_ref, g_ref, w_ref, cos_ref, sin_ref, dkb_ref, gcb_ref, bd_ref,
                 q_ref, k_ref, v_ref, gsw_ref, p_ref, ga_ref, gb_ref, sb_ref,
                 state_ref, kf32_ref, *, tile):
    n_chunks = tile // CHUNK

    @pl.when(pl.program_id(1) == 0)
    def _():
        state_ref[...] = jnp.zeros_like(state_ref)

    x = x_ref[...]
    shift = mod_ref[0:1, :]
    scale = mod_ref[1:2, :]
    h = (x * _rms_scale(x) * g_ref[...]) * (1.0 + scale) + shift
    hb = h.astype(_BF16)

    def proj(off, width):
        return jnp.dot(hb, w_ref[:, off:off + width], preferred_element_type=_F32)

    lane = lax.broadcasted_iota(jnp.int32, (tile, 128), 1)
    first_half = (lane % QK_DIM) < (QK_DIM // 2)
    cos = cos_ref[...]
    sin = sin_ref[...]
    for j in range(RET_QK // 128):
        qj = proj(OFF_Q + 128 * j, 128)
        q_ref[:, 128 * j:128 * (j + 1)] = _rotary(qj, cos, sin, first_half).astype(_BF16)
        kj = proj(OFF_K + 128 * j, 128)
        kj = _rotary(kj, cos, sin, first_half) * (QK_DIM ** -0.5)
        k_ref[:, 128 * j:128 * (j + 1)] = kj.astype(_BF16)
        kf32_ref[:, 128 * j:128 * (j + 1)] = kj
    for j in range(RET_V // 256):
        v_ref[:, 256 * j:256 * (j + 1)] = proj(OFF_V + 256 * j, 256).astype(_BF16)
        g = proj(OFF_GSW + 256 * j, 256)
        gsw_ref[:, 256 * j:256 * (j + 1)] = (g * _sigmoid(g)).astype(_BF16)
        ga_ref[:, 256 * j:256 * (j + 1)] = _sigmoid(proj(OFF_GA + 256 * j, 256)).astype(_BF16)
        gb_ref[:, 256 * j:256 * (j + 1)] = _sigmoid(proj(OFF_GB + 256 * j, 256)).astype(_BF16)
    for j in range(POOL_WIDTH // 256):
        p_ref[:, 256 * j:256 * (j + 1)] = proj(OFF_P + 256 * j, 256).astype(_BF16)

    bd = bd_ref[...]
    for c in reversed(range(n_chunks)):
        rows = slice(c * CHUNK, (c + 1) * CHUNK)
        for pr in range(N_PAIRS):
            cols = slice(128 * pr, 128 * (pr + 1))
            s_old = state_ref[pr]
            sb_ref[c, pr] = s_old.astype(_BF16)
            kd = (kf32_ref[rows, cols] * dkb_ref[:, cols]).astype(_BF16)
            vp = v_ref[rows, 256 * pr:256 * (pr + 1)]
            upd = lax.dot_general(kd, vp, (((0,), (0,)), ((), ())), preferred_element_type=_F32)
            state_ref[pr] = s_old * gcb_ref[pr] + upd * bd


def _proj_call(x, mod, g_mix, w_in_b, cos_t, sin_t, dkb, gcb, bd, *, tile):
    b, s, _ = x.shape
    nt = s // tile
    nc = s // CHUNK
    cpt = tile // CHUNK

    def rev(bi, j):
        return (bi, nt - 1 - j, 0)

    def tok_spec(width):
        return pl.BlockSpec((None, tile, width), rev)

    out_shape = (
        jax.ShapeDtypeStruct((b, s, RET_QK), _BF16),
        jax.ShapeDtypeStruct((b, s, RET_QK), _BF16),
        jax.ShapeDtypeStruct((b, s, RET_V), _BF16),
        jax.ShapeDtypeStruct((b, s, RET_V), _BF16),
        jax.ShapeDtypeStruct((b, s, POOL_WIDTH), _BF16),
        jax.ShapeDtypeStruct((b, s, D_MODEL), _BF16),
        jax.ShapeDtypeStruct((b, s, D_MODEL), _BF16),
        jax.ShapeDtypeStruct((b, nc, N_PAIRS, 2 * QK_DIM, 2 * V_DIM), _BF16),
    )
    out_specs = (
        tok_spec(RET_QK), tok_spec(RET_QK), tok_spec(RET_V), tok_spec(RET_V),
        tok_spec(POOL_WIDTH), tok_spec(D_MODEL), tok_spec(D_MODEL),
        pl.BlockSpec((None, cpt, N_PAIRS, 2 * QK_DIM, 2 * V_DIM),
                     lambda bi, j: (bi, nt - 1 - j, 0, 0, 0)),
    )
    in_specs = [
        tok_spec(D_MODEL),
        pl.BlockSpec((None, N_MOD, D_MODEL), lambda bi, j: (bi, 0, 0)),
        _const_spec((1, D_MODEL)),
        _const_spec((D_MODEL, IN_WIDTH)),
        pl.BlockSpec((tile, 128), lambda bi, j: (nt - 1 - j, 0)),
        pl.BlockSpec((tile, 128), lambda bi, j: (nt - 1 - j, 0)),
        _const_spec((CHUNK, RET_QK)),
        _const_spec((N_PAIRS, 2 * QK_DIM, 2 * V_DIM)),
        _const_spec((2 * QK_DIM, 2 * V_DIM)),
    ]
    return pl.pallas_call(
        functools.partial(_proj_kernel, tile=tile),
        grid=(b, nt),
        in_specs=in_specs,
        out_specs=out_specs,
        out_shape=out_shape,
        scratch_shapes=[pltpu.VMEM((N_PAIRS, 2 * QK_DIM, 2 * V_DIM), _F32),
                        pltpu.VMEM((tile, RET_QK), _F32)],
        compiler_params=pltpu.CompilerParams(
            dimension_semantics=("arbitrary", "arbitrary"),
            vmem_limit_bytes=VMEM_LIMIT_BYTES),
        name="proj",
    )(x, mod, g_mix, w_in_b, cos_t, sin_t, dkb, gcb, bd)


def _shift_rows(x, k):
    return pltpu.roll(x, k % x.shape[0], axis=0)


def _mix_kernel(x_ref, mod_ref, q_ref, k_ref, v_ref, gsw_ref, p_ref, pprev_ref, pnext_ref,
                ga_ref, gb_ref, sb_ref, dpair_ref, dqf_ref, dqb_ref, dkf_ref, gcf_ref, bd_ref,
                gain_ref, wpool_ref, pscale_ref, wout_ref,
                o_ref, state_ref, mb_ref, merged_ref, *, tile, seq_len):
    n_chunks = tile // CHUNK
    t_idx = pl.program_id(1)

    @pl.when(t_idx == 0)
    def _():
        state_ref[...] = jnp.zeros_like(state_ref)

    ext = tile + 2 * HALO
    prev_ok = (t_idx > 0).astype(_F32)
    next_ok = (t_idx < pl.num_programs(1) - 1).astype(_F32)
    row = lax.broadcasted_iota(jnp.int32, (tile, POOL_GROUP_IN), 0) + t_idx * tile
    for g, w in enumerate(POOL_WINDOWS):
        cols = slice(POOL_GROUP_IN * g, POOL_GROUP_IN * (g + 1))
        pg = p_ref[:, cols].astype(_F32)
        prev = pprev_ref[:, cols].astype(_F32)[BF16_ROWS - HALO:, :] * prev_ok
        nxt = pnext_ref[:, cols].astype(_F32)[:HALO, :] * next_ok
        pe = jnp.concatenate([prev, pg, nxt], axis=0)
        acc = pe
        span = 1
        while span < w:
            acc = acc + _shift_rows(acc, span)
            span *= 2
        acc = _shift_rows(acc, ext - (w // 2 - 1))[HALO:HALO + tile, :]
        cnt = (jnp.minimum(row + w // 2, seq_len) - jnp.maximum(row - w // 2, 0)).astype(_F32)
        pooled = (acc / cnt - pg).astype(_BF16)
        ocols = slice(POOL_GROUP_OUT * g, POOL_GROUP_OUT * (g + 1))
        bb = jnp.dot(pooled, wpool_ref[g], preferred_element_type=_F32) * pscale_ref[:, ocols]
        mb_ref[:, ocols] = gb_ref[:, ocols].astype(_F32) * bb

    lane = lax.broadcasted_iota(jnp.int32, (CHUNK, 2 * QK_DIM), 1)
    head0 = lane < QK_DIM
    bd = bd_ref[...]
    zeros_v = jnp.zeros((CHUNK, V_DIM), _BF16)
    for c in range(n_chunks):
        rows = slice(c * CHUNK, (c + 1) * CHUNK)
        for pr in range(N_PAIRS):
            cols = slice(128 * pr, 128 * (pr + 1))
            vcols = slice(256 * pr, 256 * (pr + 1))
            qp = q_ref[rows, cols]
            kp = k_ref[rows, cols]
            vp = v_ref[rows, vcols]
            qf32 = qp.astype(_F32)
            kf32 = kp.astype(_F32)
            kbig = jnp.concatenate([jnp.where(head0, kf32, 0.0).astype(_BF16),
                                    jnp.where(head0, 0.0, kf32).astype(_BF16)], axis=0)
            scores = lax.dot_general(qp, kbig, (((1,), (1,)), ((), ())),
                                     preferred_element_type=_F32)
            pmat = (scores * dpair_ref[pr]).astype(_BF16)
            qf = (qf32 * dqf_ref[:, cols]).astype(_BF16)
            qb = (qf32 * dqb_ref[:, cols]).astype(_BF16)
            lhs = jnp.concatenate([pmat, qf, qb], axis=1)
            vbd = jnp.concatenate(
                [jnp.concatenate([vp[:, :V_DIM], zeros_v], axis=1),
                 jnp.concatenate([zeros_v, vp[:, V_DIM:]], axis=1)], axis=0)
            s_f = state_ref[pr]
            rhs = jnp.concatenate([vbd, s_f.astype(_BF16), sb_ref[c, pr]], axis=0)
            o = jnp.dot(lhs, rhs, preferred_element_type=_F32)
            kd = (kf32 * dkf_ref[:, cols]).astype(_BF16)
            upd = lax.dot_general(kd, vp, (((0,), (0,)), ((), ())), preferred_element_type=_F32)
            state_ref[pr] = s_f * gcf_ref[pr] + upd * bd
            normed = []
            for hh in range(2):
                oh = o[:, V_DIM * hh:V_DIM * (hh + 1)]
                d = oh - jnp.mean(oh, axis=-1, keepdims=True)
                var = jnp.mean(d * d, axis=-1, keepdims=True)
                normed.append(d * lax.rsqrt(var + EPS))
            ret = jnp.concatenate(normed, axis=1) * gain_ref[:, vcols]
            branch_a = gsw_ref[rows, vcols].astype(_F32) * ret
            merged = ga_ref[rows, vcols].astype(_F32) * branch_a + mb_ref[rows, vcols]
            merged_ref[rows, vcols] = merged.astype(_BF16)

    mix = jnp.dot(merged_ref[...], wout_ref[...], preferred_element_type=_F32)
    gate1 = mod_ref[2:3, :]
    o_ref[...] = x_ref[...] + gate1 * mix


def _mix_call(x, mod, q, k, v, gsw, p, ga, gb, sb, dpair, dqf, dqb, dkf, gcf, bd,
              gain, w_pool_b, pool_scale, w_out_b, *, tile):
    b, s, _ = x.shape
    nt = s // tile
    cpt = tile // CHUNK
    hpt = tile // BF16_ROWS
    n_halo = s // BF16_ROWS

    def tok(bi, j):
        return (bi, j, 0)

    def tok_spec(width):
        return pl.BlockSpec((None, tile, width), tok)

    in_specs = [
        tok_spec(D_MODEL),
        pl.BlockSpec((None, N_MOD, D_MODEL), lambda bi, j: (bi, 0, 0)),
        tok_spec(RET_QK), tok_spec(RET_QK), tok_spec(RET_V), tok_spec(RET_V),
        tok_spec(POOL_WIDTH),
        pl.BlockSpec((None, BF16_ROWS, POOL_WIDTH),
                     lambda bi, j: (bi, jnp.maximum(j * hpt - 1, 0), 0)),
        pl.BlockSpec((None, BF16_ROWS, POOL_WIDTH),
                     lambda bi, j: (bi, jnp.minimum((j + 1) * hpt, n_halo - 1), 0)),
        tok_spec(D_MODEL), tok_spec(D_MODEL),
        pl.BlockSpec((None, cpt, N_PAIRS, 2 * QK_DIM, 2 * V_DIM), lambda bi, j: (bi, j, 0, 0, 0)),
        _const_spec((N_PAIRS, CHUNK, 2 * CHUNK)),
        _const_spec((CHUNK, RET_QK)), _const_spec((CHUNK, RET_QK)), _const_spec((CHUNK, RET_QK)),
        _const_spec((N_PAIRS, 2 * QK_DIM, 2 * V_DIM)),
        _const_spec((2 * QK_DIM, 2 * V_DIM)),
        _const_spec((1, RET_V)),
        _const_spec((N_POOL_GROUPS, POOL_GROUP_IN, POOL_GROUP_OUT)),
        _const_spec((1, D_MODEL)),
        _const_spec((D_MODEL, D_MODEL)),
    ]
    return pl.pallas_call(
        functools.partial(_mix_kernel, tile=tile, seq_len=s),
        grid=(b, nt),
        in_specs=in_specs,
        out_specs=tok_spec(D_MODEL),
        out_shape=jax.ShapeDtypeStruct((b, s, D_MODEL), _F32),
        scratch_shapes=[pltpu.VMEM((N_PAIRS, 2 * QK_DIM, 2 * V_DIM), _F32),
                        pltpu.VMEM((tile, D_MODEL), _F32),
                        pltpu.VMEM((tile, D_MODEL), _BF16)],
        compiler_params=pltpu.CompilerParams(
            dimension_semantics=("arbitrary", "arbitrary"),
            vmem_limit_bytes=VMEM_LIMIT_BYTES),
        name="mix",
    )(x, mod, q, k, v, gsw, p, p, p, ga, gb, sb, dpair, dqf, dqb, dkf, gcf, bd,
      gain, w_pool_b, pool_scale, w_out_b)


def _gelu_tanh(x):
    c = math.sqrt(2.0 / math.pi)
    return x * (0.5 * (1.0 + jnp.tanh(c * (x + 0.044715 * (x * x * x)))))


def _ffn_kernel(x_ref, xprev_ref, xnext_ref, mod_ref, g_ref, wup_ref, cw_ref, cb_ref, wdown_ref,
                gfin_ref, o_ref, act_ref, *, tile):
    t_idx = pl.program_id(1)
    ext = tile + 2 * HALO
    shift = mod_ref[3:4, :]
    scale = mod_ref[4:5, :]
    gate = mod_ref[5:6, :]

    x = x_ref[...]
    xe = jnp.concatenate([xprev_ref[...], x, xnext_ref[...]], axis=0)
    h = (xe * _rms_scale(xe) * g_ref[...]) * (1.0 + scale) + shift
    row = lax.broadcasted_iota(jnp.int32, (ext, 1), 0)
    valid = jnp.logical_and(jnp.logical_or(row >= HALO, t_idx > 0),
                            jnp.logical_or(row < HALO + tile, t_idx < pl.num_programs(1) - 1))
    hb = jnp.where(valid, h, 0.0).astype(_BF16)

    def conv(u, cols):
        w0 = cw_ref[0:1, cols]
        w1 = cw_ref[1:2, cols]
        w2 = cw_ref[2:3, cols]
        out = (_shift_rows(u, 1) * w0 + u * w1 + _shift_rows(u, ext - 1) * w2)
        return out[HALO:HALO + tile, :] + cb_ref[:, cols]

    for j in range(D_FF // FF_BLOCK):
        ca = slice(FF_BLOCK * j, FF_BLOCK * (j + 1))
        cl = slice(D_FF + FF_BLOCK * j, D_FF + FF_BLOCK * (j + 1))
        ua = jnp.dot(hb, wup_ref[:, ca], preferred_element_type=_F32)
        ul = jnp.dot(hb, wup_ref[:, cl], preferred_element_type=_F32)
        act_ref[:, ca] = (_gelu_tanh(conv(ua, ca)) * conv(ul, cl)).astype(_BF16)

    ffn = jnp.dot(act_ref[...], wdown_ref[...], preferred_element_type=_F32)
    y = x + gate * ffn
    o_ref[...] = y * _rms_scale(y) * gfin_ref[...]


def _ffn_call(x1, mod, g_ffn, w_up_b, conv_w, conv_b, w_down_b, g_final, *, tile):
    b, s, _ = x1.shape
    nt = s // tile
    hpt = tile // HALO
    n_halo = s // HALO

    tok_spec = pl.BlockSpec((None, tile, D_MODEL), lambda bi, j: (bi, j, 0))
    in_specs = [
        tok_spec,
        pl.BlockSpec((None, HALO, D_MODEL), lambda bi, j: (bi, jnp.maximum(j * hpt - 1, 0), 0)),
        pl.BlockSpec((None, HALO, D_MODEL),
                     lambda bi, j: (bi, jnp.minimum((j + 1) * hpt, n_halo - 1), 0)),
        pl.BlockSpec((None, N_MOD, D_MODEL), lambda bi, j: (bi, 0, 0)),
        _const_spec((1, D_MODEL)),
        _const_spec((D_MODEL, 2 * D_FF)),
        _const_spec((3, 2 * D_FF)),
        _const_spec((1, 2 * D_FF)),
        _const_spec((D_FF, D_MODEL)),
        _const_spec((1, D_MODEL)),
    ]
    return pl.pallas_call(
        functools.partial(_ffn_kernel, tile=tile),
        grid=(b, nt),
        in_specs=in_specs,
        out_specs=tok_spec,
        out_shape=jax.ShapeDtypeStruct((b, s, D_MODEL), _F32),
        scratch_shapes=[pltpu.VMEM((tile, D_FF), _BF16)],
        compiler_params=pltpu.CompilerParams(
            dimension_semantics=("arbitrary", "arbitrary"),
            vmem_limit_bytes=VMEM_LIMIT_BYTES),
        name="ffn",
    )(x1, x1, x1, mod, g_ffn, w_up_b, conv_w, conv_b, w_down_b, g_final)


def _rope_tables(s):
    half = QK_DIM // 2
    inv_freq = ROPE_BASE ** (-jnp.arange(half, dtype=_F32) / half)
    ang = jnp.arange(s, dtype=_F32)[:, None] * inv_freq[None, :]
    cos = jnp.tile(jnp.cos(ang), (1, 4))
    sin = jnp.tile(jnp.sin(ang), (1, 4))
    first_half = (jnp.arange(128) % QK_DIM) < half
    return cos, jnp.where(first_half[None, :], -sin, sin)


def _decay_tables(ret_decay_f, ret_decay_b):
    lg_f = jax.nn.log_sigmoid(ret_decay_f.astype(_F32))
    lg_b = jax.nn.log_sigmoid(ret_decay_b.astype(_F32))
    idx = jnp.arange(CHUNK, dtype=_F32)
    diff = idx[:, None] - idx[None, :]
    causal = diff >= 0.0
    d_f = jnp.exp(jnp.where(causal, diff, 0.0)[None] * lg_f[:, None, None])
    d_b = jnp.exp(jnp.where(causal, 0.0, -diff)[None] * lg_b[:, None, None])
    d = jnp.where(causal[None], d_f, d_b)
    dpair = d.reshape(N_PAIRS, 2, CHUNK, CHUNK).transpose(0, 2, 1, 3).reshape(N_PAIRS, CHUNK, 2 * CHUNK)

    def per_lane(power, lg):
        return jnp.repeat(jnp.exp(power[:, None] * lg[None, :]), QK_DIM, axis=1)

    dqf = per_lane(idx + 1.0, lg_f)
    dqb = per_lane(CHUNK - idx, lg_b)
    dkf = per_lane(CHUNK - 1.0 - idx, lg_f)
    dkb = per_lane(idx, lg_b)
    r_head = jnp.arange(2 * QK_DIM) // QK_DIM
    c_head = jnp.arange(2 * V_DIM) // V_DIM
    bd = (r_head[:, None] == c_head[None, :]).astype(_F32)

    def chunk_decay(lg):
        gc = jnp.exp(CHUNK * lg).reshape(N_PAIRS, 2)
        return gc[:, r_head][:, :, None] * bd[None]

    return dpair, dqf, dqb, dkf, dkb, chunk_decay(lg_f), chunk_decay(lg_b), bd


def _encoder(x, c, w, *, tile):
    b, s, _ = x.shape
    mod = _ada_call(c, w["w_ada"], w["b_ada"]).reshape(b, N_MOD, D_MODEL)
    cos_t, sin_t = _rope_tables(s)
    q, k, v, gsw, p, ga, gb, sb = _proj_call(
        x, mod, w["g_mix"], w["w_in"], cos_t, sin_t, w["dkb"], w["gcb"], w["bd"], tile=tile)
    x1 = _mix_call(x, mod, q, k, v, gsw, p, ga, gb, sb, w["dpair"], w["dqf"], w["dqb"], w["dkf"],
                   w["gcf"], w["bd"], w["gain"], w["w_pool"], w["pool_scale"], w["w_out"], tile=tile)
    return _ffn_call(x1, mod, w["g_ffn"], w["w_up"], w["conv_w"], w["conv_b"], w["w_down"],
                     w["g_final"], tile=tile)


def _prepare_weights(w_ada, b_ada, g_mix, w_in, ret_decay_f, ret_decay_b, ret_gn_gain, w_pool_grp,
                     pool_scale, w_out, g_ffn, w_up, conv_w, conv_b, w_down, g_final):
    dpair, dqf, dqb, dkf, dkb, gcf, gcb, bd = _decay_tables(ret_decay_f[0], ret_decay_b[0])
    return dict(
        w_ada=w_ada[0].astype(_BF16), b_ada=b_ada[0],
        g_mix=g_mix[0].reshape(1, D_MODEL), w_in=w_in[0].astype(_BF16),
        dpair=dpair, dqf=dqf, dqb=dqb, dkf=dkf, dkb=dkb, gcf=gcf, gcb=gcb, bd=bd,
        gain=ret_gn_gain[0].reshape(1, RET_V), w_pool=w_pool_grp[0].astype(_BF16),
        pool_scale=pool_scale[0].reshape(1, D_MODEL), w_out=w_out[0].astype(_BF16),
        g_ffn=g_ffn[0].reshape(1, D_MODEL), w_up=w_up[0].astype(_BF16),
        conv_w=conv_w[0], conv_b=conv_b[0].reshape(1, 2 * D_FF),
        w_down=w_down[0].astype(_BF16), g_final=g_final.reshape(1, D_MODEL))


def kernel(x_prompt, x_sample, c_prompt, c_sample, w_ada, b_ada, g_mix, w_in, ret_decay_f,
           ret_decay_b, ret_gn_gain, w_pool_grp, pool_scale, w_out, g_ffn, w_up, conv_w, conv_b,
           w_down, g_final):
    assert w_ada.shape[0] == 1, "single-layer trunk"
    w = _prepare_weights(w_ada, b_ada, g_mix, w_in, ret_decay_f, ret_decay_b, ret_gn_gain,
                         w_pool_grp, pool_scale, w_out, g_ffn, w_up, conv_w, conv_b, w_down, g_final)
    tile = min(512, x_prompt.shape[1])
    y_prompt = _encoder(x_prompt, c_prompt, w, tile=tile)
    y_sample = _encoder(x_sample, c_sample, w, tile=min(512, x_sample.shape[1]))
    return (y_prompt, y_sample)
```

```python
import functools
import math

import jax
import jax.numpy as jnp
from jax import lax
from jax.experimental import pallas as pl
from jax.experimental.pallas import tpu as pltpu

D_MODEL = 1024
N_HEADS = 8
QK_DIM = 64
V_DIM = 128
RET_QK = N_HEADS * QK_DIM
RET_V = N_HEADS * V_DIM
N_PAIRS = N_HEADS // 2
CHUNK = 128
ROPE_BASE = 10000.0
POOL_WINDOWS = (2, 4, 8, 16)
N_POOL_GROUPS = 4
POOL_WIDTH = 512
POOL_GROUP_IN = 128
POOL_GROUP_OUT = 256
D_FF = 2816
N_MOD = 6
EPS = 1e-6
IN_WIDTH = 2 * RET_QK + 2 * RET_V + POOL_WIDTH + 2 * D_MODEL
OFF_Q, OFF_K, OFF_V, OFF_GSW, OFF_P, OFF_GA, OFF_GB = 0, 512, 1024, 2048, 3072, 3584, 4608

HALO = 8
FF_BLOCK = 256
FF_TAIL_BLOCKS = 2
VMEM_LIMIT_BYTES = 56 * 1024 * 1024

_F32 = jnp.float32
_BF16 = jnp.bfloat16


def _const_spec(shape):
    nd = len(shape)
    return pl.BlockSpec(shape, lambda *_: (0,) * nd, pipeline_mode=pl.Buffered(1))


def _sigmoid(x):
    return 1.0 / (1.0 + jnp.exp(-x))


def _rms_scale(x):
    return lax.rsqrt(jnp.mean(x * x, axis=-1, keepdims=True) + EPS)


def _dot(a, b):
    return jnp.dot(a, b, preferred_element_type=_F32)


def _pair_diag(upd):
    return upd[:QK_DIM, :V_DIM], upd[QK_DIM:, V_DIM:]


def _shift_rows(x, k):
    k = k % x.shape[0]
    return x if k == 0 else pltpu.roll(x, k, axis=0)


def _halo_rows_valid(tile, has_prev, has_next):
    row = lax.broadcasted_iota(jnp.int32, (tile + 2 * HALO, 1), 0)
    return jnp.logical_and(jnp.logical_or(row >= HALO, has_prev),
                           jnp.logical_or(row < HALO + tile, has_next))


def _halo_specs(tile, seq_len, tile_of_step):
    hpt = tile // HALO
    n_halo = seq_len // HALO
    prev = pl.BlockSpec((None, HALO, D_MODEL),
                        lambda bi, j: (bi, jnp.maximum(tile_of_step(j) * hpt - 1, 0), 0))
    nxt = pl.BlockSpec((None, HALO, D_MODEL),
                       lambda bi, j: (bi, jnp.minimum((tile_of_step(j) + 1) * hpt, n_halo - 1), 0))
    return prev, nxt


def _ada_kernel(c_ref, w_ref, b_ref, o_ref):
    c = c_ref[...]
    a = (c * _sigmoid(c)).astype(_BF16)
    o_ref[...] = _dot(a, w_ref[...]) + b_ref[...]


def _ada_call(c, w_ada_b, b_ada):
    b = c.shape[0]
    n = w_ada_b.shape[1]
    nb = D_MODEL
    return pl.pallas_call(
        _ada_kernel,
        grid=(n // nb,),
        in_specs=[pl.BlockSpec((b, D_MODEL), lambda j: (0, 0)),
                  pl.BlockSpec((D_MODEL, nb), lambda j: (0, j)),
                  pl.BlockSpec((1, nb), lambda j: (0, j))],
        out_specs=pl.BlockSpec((b, nb), lambda j: (0, j)),
        out_shape=jax.ShapeDtypeStruct((b, n), _F32),
        name="ada",
    )(c, w_ada_b, b_ada.reshape(1, n))


def _rotary(t, cos, sin_signed, first_half):
    partner = jnp.where(first_half, pltpu.roll(t, 96, axis=1), pltpu.roll(t, 32, axis=1))
    return t * cos + partner * sin_signed


def _proj_kernel(x_ref, xprev_ref, xnext_ref, mod_ref, g_ref, w_ref, cos_ref, sin_ref, invc_ref,
                 dkbt_ref, gcb_ref, gain_ref, pscale_ref,
                 q_ref, klot_ref, khit_ref, v_ref, gs_ref, pooled_ref, gbs_ref, sb_ref,
                 state_ref, kt_ref, *, tile):
    n_chunks = tile // CHUNK
    ext = tile + 2 * HALO
    step = pl.program_id(1)
    t_idx = pl.num_programs(1) - 1 - step

    @pl.when(step == 0)
    def _():
        state_ref[...] = jnp.zeros_like(state_ref)

    shift = mod_ref[0:1, :]
    scale = mod_ref[1:2, :]
    xe = jnp.concatenate([xprev_ref[...], x_ref[...], xnext_ref[...]], axis=0)
    h = (xe * _rms_scale(xe) * g_ref[...]) * (1.0 + scale) + shift
    h = jnp.where(_halo_rows_valid(tile, t_idx > 0, t_idx < pl.num_programs(1) - 1), h, 0.0)
    hb_ext = h.astype(_BF16)
    hb = h[HALO:HALO + tile, :].astype(_BF16)

    def proj(off, width):
        return _dot(hb, w_ref[:, off:off + width])

    for j in range(POOL_WIDTH // 256):
        pe2 = _dot(hb_ext, w_ref[:, OFF_P + 256 * j:OFF_P + 256 * (j + 1)])
        for hh in range(2):
            g = 2 * j + hh
            w = POOL_WINDOWS[g]
            pe = pe2[:, 128 * hh:128 * (hh + 1)]
            acc = pe
            span = 1
            while span < w:
                acc = acc + _shift_rows(acc, span)
                span *= 2
            acc = _shift_rows(acc, ext - (w // 2 - 1))[HALO:HALO + tile, :]
            pooled = acc * invc_ref[g] - pe[HALO:HALO + tile, :]
            pooled_ref[:, POOL_GROUP_IN * g:POOL_GROUP_IN * (g + 1)] = pooled.astype(_BF16)

    lane = lax.broadcasted_iota(jnp.int32, (tile, 128), 1)
    first_half = (lane % QK_DIM) < (QK_DIM // 2)
    head0_rows = lax.broadcasted_iota(jnp.int32, (128, tile), 0) < QK_DIM
    cos = cos_ref[...]
    sin = sin_ref[...]
    for j in range(RET_QK // 256):
        q2 = proj(OFF_Q + 256 * j, 256)
        k2 = proj(OFF_K + 256 * j, 256)
        for hh in range(2):
            cols = slice(256 * j + 128 * hh, 256 * j + 128 * (hh + 1))
            half = slice(128 * hh, 128 * (hh + 1))
            q_ref[:, cols] = _rotary(q2[:, half], cos, sin, first_half).astype(_BF16)
            kt = (_rotary(k2[:, half], cos, sin, first_half) * (QK_DIM ** -0.5)).T
            kt_ref[cols, :] = kt
            klot_ref[cols, :] = jnp.where(head0_rows, kt, 0.0).astype(_BF16)
            khit_ref[cols, :] = jnp.where(head0_rows, 0.0, kt).astype(_BF16)

    for j in range(RET_V // 256):
        cols = slice(256 * j, 256 * (j + 1))
        v_ref[:, cols] = proj(OFF_V + 256 * j, 256).astype(_BF16)
        g = proj(OFF_GSW + 256 * j, 256)
        ga = _sigmoid(proj(OFF_GA + 256 * j, 256))
        gs_ref[:, cols] = (ga * ((g * _sigmoid(g)) * gain_ref[:, cols])).astype(_BF16)
        gb = _sigmoid(proj(OFF_GB + 256 * j, 256))
        gbs_ref[:, cols] = (gb * pscale_ref[:, cols]).astype(_BF16)

    for c in reversed(range(n_chunks)):
        toks = slice(c * CHUNK, (c + 1) * CHUNK)
        for pr in range(N_PAIRS):
            prow = slice(128 * pr, 128 * (pr + 1))
            kdt = (kt_ref[prow, toks] * dkbt_ref[prow, :]).astype(_BF16)
            upd = _pair_diag(_dot(kdt, v_ref[toks, 256 * pr:256 * (pr + 1)]))
            for hh in range(2):
                hd = 2 * pr + hh
                s_old = state_ref[hd]
                sb_ref[c, hd] = s_old.astype(_BF16)
                state_ref[hd] = s_old * gcb_ref[hd] + upd[hh]


def _proj_call(x, mod, g_mix, w_in_b, cos_t, sin_t, invc, dkbt, gcb, gain, pool_scale, *, tile):
    b, s, _ = x.shape
    nt = s // tile
    nc = s // CHUNK
    cpt = tile // CHUNK

    def tile_of_step(j):
        return nt - 1 - j

    def tok_spec(width):
        return pl.BlockSpec((None, tile, width), lambda bi, j: (bi, tile_of_step(j), 0))

    def kt_spec():
        return pl.BlockSpec((None, RET_QK, tile), lambda bi, j: (bi, 0, tile_of_step(j)))

    def edge_variant(bi, j):
        t = tile_of_step(j)
        return ((t == 0).astype(jnp.int32) + 2 * (t == nt - 1).astype(jnp.int32), 0, 0, 0)

    out_shape = (
        jax.ShapeDtypeStruct((b, s, RET_QK), _BF16),
        jax.ShapeDtypeStruct((b, RET_QK, s), _BF16),
        jax.ShapeDtypeStruct((b, RET_QK, s), _BF16),
        jax.ShapeDtypeStruct((b, s, RET_V), _BF16),
        jax.ShapeDtypeStruct((b, s, RET_V), _BF16),
        jax.ShapeDtypeStruct((b, s, POOL_WIDTH), _BF16),
        jax.ShapeDtypeStruct((b, s, D_MODEL), _BF16),
        jax.ShapeDtypeStruct((b, nc, N_HEADS, QK_DIM, V_DIM), _BF16),
    )
    out_specs = (
        tok_spec(RET_QK), kt_spec(), kt_spec(), tok_spec(RET_V), tok_spec(RET_V),
        tok_spec(POOL_WIDTH), tok_spec(D_MODEL),
        pl.BlockSpec((None, cpt, N_HEADS, QK_DIM, V_DIM),
                     lambda bi, j: (bi, tile_of_step(j), 0, 0, 0)),
    )
    prev_spec, next_spec = _halo_specs(tile, s, tile_of_step)
    in_specs = [
        tok_spec(D_MODEL), prev_spec, next_spec,
        pl.BlockSpec((None, N_MOD, D_MODEL), lambda bi, j: (bi, 0, 0)),
        _const_spec((1, D_MODEL)),
        _const_spec((D_MODEL, IN_WIDTH)),
        pl.BlockSpec((tile, 128), lambda bi, j: (tile_of_step(j), 0)),
        pl.BlockSpec((tile, 128), lambda bi, j: (tile_of_step(j), 0)),
        pl.BlockSpec((None, N_POOL_GROUPS, tile, POOL_GROUP_IN), edge_variant),
        _const_spec((RET_QK, CHUNK)),
        _const_spec((N_HEADS, 1, V_DIM)),
        _const_spec((1, RET_V)),
        _const_spec((1, D_MODEL)),
    ]
    return pl.pallas_call(
        functools.partial(_proj_kernel, tile=tile),
        grid=(b, nt),
        in_specs=in_specs,
        out_specs=out_specs,
        out_shape=out_shape,
        scratch_shapes=[pltpu.VMEM((N_HEADS, QK_DIM, V_DIM), _F32),
                        pltpu.VMEM((RET_QK, tile), _F32)],
        compiler_params=pltpu.CompilerParams(
            dimension_semantics=("arbitrary", "arbitrary"),
            vmem_limit_bytes=VMEM_LIMIT_BYTES),
        name="proj",
    )(x, x, x, mod, g_mix, w_in_b, cos_t, sin_t, invc, dkbt, gcb, gain, pool_scale)


def _mix_kernel(x_ref, mod_ref, q_ref, klot_ref, khit_ref, v_ref, gs_ref, pooled_ref, gbs_ref, sb_ref,
                dpair_ref, dqf_ref, dqb_ref, dkft_ref, gcf_ref, wpool_ref, wout_ref,
                o_ref, state_ref, mb_ref, merged_ref, *, tile):
    n_chunks = tile // CHUNK

    @pl.when(pl.program_id(1) == 0)
    def _():
        state_ref[...] = jnp.zeros_like(state_ref)

    for g in range(N_POOL_GROUPS):
        ocols = slice(POOL_GROUP_OUT * g, POOL_GROUP_OUT * (g + 1))
        bb = _dot(pooled_ref[:, POOL_GROUP_IN * g:POOL_GROUP_IN * (g + 1)], wpool_ref[g])
        mb_ref[:, ocols] = gbs_ref[:, ocols].astype(_F32) * bb

    zeros_v = jnp.zeros((CHUNK, V_DIM), _BF16)
    zeros_s = jnp.zeros((QK_DIM, V_DIM), _BF16)

    def pair_state(sa, sb):
        return jnp.concatenate([jnp.concatenate([sa, zeros_s], axis=1),
                                jnp.concatenate([zeros_s, sb], axis=1)], axis=0)

    for c in range(n_chunks):
        toks = slice(c * CHUNK, (c + 1) * CHUNK)
        for pr in range(N_PAIRS):
            cols = slice(128 * pr, 128 * (pr + 1))
            vcols = slice(256 * pr, 256 * (pr + 1))
            ha, hb = 2 * pr, 2 * pr + 1
            qp = q_ref[toks, cols]
            klot = klot_ref[cols, toks]
            khit = khit_ref[cols, toks]
            vp = v_ref[toks, vcols]
            qf32 = qp.astype(_F32)
            scores = _dot(qp, jnp.concatenate([klot, khit], axis=1))
            pmat = (scores * dpair_ref[pr]).astype(_BF16)
            qf = (qf32 * dqf_ref[:, cols]).astype(_BF16)
            qb = (qf32 * dqb_ref[:, cols]).astype(_BF16)
            lhs = jnp.concatenate([pmat, qf, qb], axis=1)
            vbd = jnp.concatenate(
                [jnp.concatenate([vp[:, :V_DIM], zeros_v], axis=1),
                 jnp.concatenate([zeros_v, vp[:, V_DIM:]], axis=1)], axis=0)
            sf_a = state_ref[ha]
            sf_b = state_ref[hb]
            rhs = jnp.concatenate([vbd,
                                   pair_state(sf_a.astype(_BF16), sf_b.astype(_BF16)),
                                   pair_state(sb_ref[c, ha], sb_ref[c, hb])], axis=0)
            o = _dot(lhs, rhs)
            kdt = ((klot + khit).astype(_F32) * dkft_ref[cols, :]).astype(_BF16)
            upd_a, upd_b = _pair_diag(_dot(kdt, vp))
            state_ref[ha] = sf_a * gcf_ref[ha] + upd_a
            state_ref[hb] = sf_b * gcf_ref[hb] + upd_b
            normed = []
            for hh in range(2):
                oh = o[:, V_DIM * hh:V_DIM * (hh + 1)]
                d = oh - jnp.mean(oh, axis=-1, keepdims=True)
                var = jnp.mean(d * d, axis=-1, keepdims=True)
                normed.append(d * lax.rsqrt(var + EPS))
            merged = (gs_ref[toks, vcols].astype(_F32) * jnp.concatenate(normed, axis=1)
                      + mb_ref[toks, vcols])
            merged_ref[toks, vcols] = merged.astype(_BF16)

    gate1 = mod_ref[2:3, :]
    o_ref[...] = x_ref[...] + gate1 * _dot(merged_ref[...], wout_ref[...])


def _mix_call(x, mod, q, klot, khit, v, gs, pooled, gbs, sb, dpair, dqf, dqb, dkft, gcf,
              w_pool_b, w_out_b, *, tile):
    b, s, _ = x.shape
    nt = s // tile
    cpt = tile // CHUNK

    def tok_spec(width):
        return pl.BlockSpec((None, tile, width), lambda bi, j: (bi, j, 0))

    def kt_spec():
        return pl.BlockSpec((None, RET_QK, tile), lambda bi, j: (bi, 0, j))

    in_specs = [
        tok_spec(D_MODEL),
        pl.BlockSpec((None, N_MOD, D_MODEL), lambda bi, j: (bi, 0, 0)),
        tok_spec(RET_QK), kt_spec(), kt_spec(), tok_spec(RET_V), tok_spec(RET_V),
        tok_spec(POOL_WIDTH), tok_spec(D_MODEL),
        pl.BlockSpec((None, cpt, N_HEADS, QK_DIM, V_DIM), lambda bi, j: (bi, j, 0, 0, 0)),
        _const_spec((N_PAIRS, CHUNK, 2 * CHUNK)),
        _const_spec((CHUNK, RET_QK)), _const_spec((CHUNK, RET_QK)), _const_spec((RET_QK, CHUNK)),
        _const_spec((N_HEADS, 1, V_DIM)),
        _const_spec((N_POOL_GROUPS, POOL_GROUP_IN, POOL_GROUP_OUT)),
        _const_spec((D_MODEL, D_MODEL)),
    ]
    return pl.pallas_call(
        functools.partial(_mix_kernel, tile=tile),
        grid=(b, nt),
        in_specs=in_specs,
        out_specs=tok_spec(D_MODEL),
        out_shape=jax.ShapeDtypeStruct((b, s, D_MODEL), _F32),
        scratch_shapes=[pltpu.VMEM((N_HEADS, QK_DIM, V_DIM), _F32),
                        pltpu.VMEM((tile, D_MODEL), _F32),
                        pltpu.VMEM((tile, D_MODEL), _BF16)],
        compiler_params=pltpu.CompilerParams(
            dimension_semantics=("arbitrary", "arbitrary"),
            vmem_limit_bytes=VMEM_LIMIT_BYTES),
        name="mix",
    )(x, mod, q, klot, khit, v, gs, pooled, gbs, sb, dpair, dqf, dqb, dkft, gcf, w_pool_b, w_out_b)


def _gelu_tanh(x):
    c = math.sqrt(2.0 / math.pi)
    half_x = 0.5 * x
    return half_x + half_x * jnp.tanh(x * (c + (c * 0.044715) * (x * x)))


def _ffn_kernel(x_ref, xprev_ref, xnext_ref, mod_ref, g_ref, wup_ref, cw_ref, cb_ref, wdown_ref,
                gfin_ref, o_ref, act_head_ref, act_tail_ref, *, tile):
    t_idx = pl.program_id(1)
    ext = tile + 2 * HALO
    shift = mod_ref[3:4, :]
    scale = mod_ref[4:5, :]
    gate = mod_ref[5:6, :]

    x = x_ref[...]
    xe = jnp.concatenate([xprev_ref[...], x, xnext_ref[...]], axis=0)
    h = (xe * _rms_scale(xe) * g_ref[...]) * (1.0 + scale) + shift
    valid = _halo_rows_valid(tile, t_idx > 0, t_idx < pl.num_programs(1) - 1)
    hb = jnp.where(valid, h, 0.0).astype(_BF16)

    def conv(u, cols):
        out = (_shift_rows(u, 1) * cw_ref[0:1, cols] + u * cw_ref[1:2, cols]
               + _shift_rows(u, ext - 1) * cw_ref[2:3, cols])
        return out[HALO:HALO + tile, :] + cb_ref[:, cols]

    n_blocks = D_FF // FF_BLOCK
    n_head = n_blocks - FF_TAIL_BLOCKS
    for j in range(n_blocks):
        ca = slice(FF_BLOCK * j, FF_BLOCK * (j + 1))
        cl = slice(D_FF + FF_BLOCK * j, D_FF + FF_BLOCK * (j + 1))
        ua = _dot(hb, wup_ref[:, ca])
        ul = _dot(hb, wup_ref[:, cl])
        act = (_gelu_tanh(conv(ua, ca)) * conv(ul, cl)).astype(_BF16)
        if j < n_head:
            act_head_ref[:, ca] = act
        else:
            jt = j - n_head
            act_tail_ref[:, FF_BLOCK * jt:FF_BLOCK * (jt + 1)] = act

    k_head = n_head * FF_BLOCK
    ffn = (_dot(act_head_ref[...], wdown_ref[:k_head, :])
           + _dot(act_tail_ref[...], wdown_ref[k_head:, :]))
    y = x + gate * ffn
    o_ref[...] = y * _rms_scale(y) * gfin_ref[...]


def _ffn_call(x1, mod, g_ffn, w_up_b, conv_w, conv_b, w_down_b, g_final, *, tile):
    b, s, _ = x1.shape
    nt = s // tile
    n_head = D_FF // FF_BLOCK - FF_TAIL_BLOCKS

    tok_spec = pl.BlockSpec((None, tile, D_MODEL), lambda bi, j: (bi, j, 0))
    prev_spec, next_spec = _halo_specs(tile, s, lambda j: j)
    in_specs = [
        tok_spec, prev_spec, next_spec,
        pl.BlockSpec((None, N_MOD, D_MODEL), lambda bi, j: (bi, 0, 0)),
        _const_spec((1, D_MODEL)),
        _const_spec((D_MODEL, 2 * D_FF)),
        _const_spec((3, 2 * D_FF)),
        _const_spec((1, 2 * D_FF)),
        _const_spec((D_FF, D_MODEL)),
        _const_spec((1, D_MODEL)),
    ]
    return pl.pallas_call(
        functools.partial(_ffn_kernel, tile=tile),
        grid=(b, nt),
        in_specs=in_specs,
        out_specs=tok_spec,
        out_shape=jax.ShapeDtypeStruct((b, s, D_MODEL), _F32),
        scratch_shapes=[pltpu.VMEM((tile, n_head * FF_BLOCK), _BF16),
                        pltpu.VMEM((tile, FF_TAIL_BLOCKS * FF_BLOCK), _BF16)],
        compiler_params=pltpu.CompilerParams(
            dimension_semantics=("arbitrary", "arbitrary"),
            vmem_limit_bytes=VMEM_LIMIT_BYTES),
        name="ffn",
    )(x1, x1, x1, mod, g_ffn, w_up_b, conv_w, conv_b, w_down_b, g_final)


def _rope_tables(s):
    half = QK_DIM // 2
    inv_freq = ROPE_BASE ** (-jnp.arange(half, dtype=_F32) / half)
    ang = jnp.arange(s, dtype=_F32)[:, None] * inv_freq[None, :]
    cos = jnp.tile(jnp.cos(ang), (1, 4))
    sin = jnp.tile(jnp.sin(ang), (1, 4))
    first_half = (jnp.arange(128) % QK_DIM) < half
    return cos, jnp.where(first_half[None, :], -sin, sin)


def _pool_inv_counts(tile):
    row = jnp.arange(tile)
    variants = []
    for v in range(4):
        per_group = []
        for w in POOL_WINDOWS:
            lo = row - w // 2
            hi = row + w // 2
            if v & 1:
                lo = jnp.maximum(lo, 0)
            if v & 2:
                hi = jnp.minimum(hi, tile)
            inv = 1.0 / (hi - lo).astype(_F32)
            per_group.append(jnp.broadcast_to(inv[:, None], (tile, POOL_GROUP_IN)))
        variants.append(jnp.stack(per_group))
    return jnp.stack(variants)


def _decay_tables(ret_decay_f, ret_decay_b):
    lg_f = jax.nn.log_sigmoid(ret_decay_f.astype(_F32))
    lg_b = jax.nn.log_sigmoid(ret_decay_b.astype(_F32))
    idx = jnp.arange(CHUNK, dtype=_F32)
    diff = idx[:, None] - idx[None, :]
    causal = diff >= 0.0
    d_f = jnp.exp(jnp.where(causal, diff, 0.0)[None] * lg_f[:, None, None])
    d_b = jnp.exp(jnp.where(causal, 0.0, -diff)[None] * lg_b[:, None, None])
    d = jnp.where(causal[None], d_f, d_b)
    dpair = d.reshape(N_PAIRS, 2, CHUNK, CHUNK).transpose(0, 2, 1, 3).reshape(N_PAIRS, CHUNK, 2 * CHUNK)

    def per_lane(power, lg):
        return jnp.repeat(jnp.exp(power[:, None] * lg[None, :]), QK_DIM, axis=1)

    def chunk_decay(lg):
        return jnp.broadcast_to(jnp.exp(CHUNK * lg)[:, None, None], (N_HEADS, 1, V_DIM))

    return dict(dpair=dpair, dqf=per_lane(idx + 1.0, lg_f), dqb=per_lane(CHUNK - idx, lg_b),
                dkft=per_lane(CHUNK - 1.0 - idx, lg_f).T, dkbt=per_lane(idx, lg_b).T,
                gcf=chunk_decay(lg_f), gcb=chunk_decay(lg_b))


def _encoder(x, c, w, *, tile):
    b, s, _ = x.shape
    mod = _ada_call(c, w["w_ada"], w["b_ada"]).reshape(b, N_MOD, D_MODEL)
    cos_t, sin_t = _rope_tables(s)
    q, klot, khit, v, gs, pooled, gbs, sb = _proj_call(
        x, mod, w["g_mix"], w["w_in"], cos_t, sin_t, _pool_inv_counts(tile), w["dkbt"], w["gcb"],
        w["gain"], w["pool_scale"], tile=tile)
    x1 = _mix_call(x, mod, q, klot, khit, v, gs, pooled, gbs, sb, w["dpair"], w["dqf"], w["dqb"],
                   w["dkft"], w["gcf"], w["w_pool"], w["w_out"], tile=tile)
    return _ffn_call(x1, mod, w["g_ffn"], w["w_up"], w["conv_w"], w["conv_b"], w["w_down"],
                     w["g_final"], tile=tile)


def _prepare_weights(w_ada, b_ada, g_mix, w_in, ret_decay_f, ret_decay_b, ret_gn_gain, w_pool_grp,
                     pool_scale, w_out, g_ffn, w_up, conv_w, conv_b, w_down, g_final):
    return dict(
        _decay_tables(ret_decay_f[0], ret_decay_b[0]),
        w_ada=w_ada[0].astype(_BF16), b_ada=b_ada[0],
        g_mix=g_mix[0].reshape(1, D_MODEL), w_in=w_in[0].astype(_BF16),
        gain=ret_gn_gain[0].reshape(1, RET_V), w_pool=w_pool_grp[0].astype(_BF16),
        pool_scale=pool_scale[0].reshape(1, D_MODEL), w_out=w_out[0].astype(_BF16),
        g_ffn=g_ffn[0].reshape(1, D_MODEL), w_up=w_up[0].astype(_BF16),
        conv_w=conv_w[0], conv_b=conv_b[0].reshape(1, 2 * D_FF),
        w_down=w_down[0].astype(_BF16), g_final=g_final.reshape(1, D_MODEL))


def kernel(x_prompt, x_sample, c_prompt, c_sample, w_ada, b_ada, g_mix, w_in, ret_decay_f,
           ret_decay_b, ret_gn_gain, w_pool_grp, pool_scale, w_out, g_ffn, w_up, conv_w, conv_b,
           w_down, g_final):
    assert w_ada.shape[0] == 1, "single-layer trunk"
    w = _prepare_weights(w_ada, b_ada, g_mix, w_in, ret_decay_f, ret_decay_b, ret_gn_gain,
                         w_pool_grp, pool_scale, w_out, g_ffn, w_up, conv_w, conv_b, w_down, g_final)
    y_prompt = _encoder(x_prompt, c_prompt, w, tile=min(512, x_prompt.shape[1]))
    y_sample = _encoder(x_sample, c_sample, w, tile=min(512, x_sample.shape[1]))
    return (y_prompt, y_sample)
```

```python
import functools
import math

import jax
import jax.numpy as jnp
from jax import lax
from jax.experimental import pallas as pl
from jax.experimental.pallas import tpu as pltpu

D_MODEL = 1024
N_HEADS = 8
QK_DIM = 64
V_DIM = 128
RET_QK = N_HEADS * QK_DIM
RET_V = N_HEADS * V_DIM
N_PAIRS = N_HEADS // 2
CHUNK = 128
ROPE_BASE = 10000.0
POOL_WINDOWS = (2, 4, 8, 16)
N_POOL_GROUPS = 4
POOL_WIDTH = 512
POOL_GROUP_IN = 128
POOL_GROUP_OUT = 256
D_FF = 2816
N_MOD = 6
EPS = 1e-6
IN_WIDTH = 2 * RET_QK + 2 * RET_V + POOL_WIDTH + 2 * D_MODEL
OFF_Q, OFF_K, OFF_V, OFF_GSW, OFF_P, OFF_GA, OFF_GB = 0, 512, 1024, 2048, 3072, 3584, 4608

HALO = 8
FF_BLOCK = 256
FF_TAIL_BLOCKS = 2
VMEM_LIMIT_BYTES = 56 * 1024 * 1024

_F32 = jnp.float32
_BF16 = jnp.bfloat16


def _const_spec(shape):
    nd = len(shape)
    return pl.BlockSpec(shape, lambda *_: (0,) * nd, pipeline_mode=pl.Buffered(1))


def _sigmoid(x):
    return 1.0 / (1.0 + jnp.exp(-x))


def _rms_scale(x):
    return lax.rsqrt(jnp.mean(x * x, axis=-1, keepdims=True) + EPS)


def _adaln_rmsnorm(v, gain, shift):
    return (v * _rms_scale(v)) * gain + shift


def _dot(a, b):
    return jnp.dot(a, b, preferred_element_type=_F32)


def _pair_diag(upd):
    return upd[:QK_DIM, :V_DIM], upd[QK_DIM:, V_DIM:]


def _shift_rows(x, k):
    k = k % x.shape[0]
    return x if k == 0 else pltpu.roll(x, k, axis=0)


def _halo_specs(tile, seq_len, tile_of_step):
    hpt = tile // HALO
    n_halo = seq_len // HALO
    prev = pl.BlockSpec((None, HALO, D_MODEL),
                        lambda bi, j: (bi, jnp.maximum(tile_of_step(j) * hpt - 1, 0), 0))
    nxt = pl.BlockSpec((None, HALO, D_MODEL),
                       lambda bi, j: (bi, jnp.minimum((tile_of_step(j) + 1) * hpt, n_halo - 1), 0))
    return prev, nxt


def _ada_kernel(c_ref, w_ref, b_ref, o_ref):
    c = c_ref[...]
    a = (c * _sigmoid(c)).astype(_BF16)
    o_ref[...] = _dot(a, w_ref[...]) + b_ref[...]


def _ada_call(c, w_ada_b, b_ada):
    b = c.shape[0]
    n = w_ada_b.shape[1]
    nb = D_MODEL
    return pl.pallas_call(
        _ada_kernel,
        grid=(n // nb,),
        in_specs=[pl.BlockSpec((b, D_MODEL), lambda j: (0, 0)),
                  pl.BlockSpec((D_MODEL, nb), lambda j: (0, j)),
                  pl.BlockSpec((1, nb), lambda j: (0, j))],
        out_specs=pl.BlockSpec((b, nb), lambda j: (0, j)),
        out_shape=jax.ShapeDtypeStruct((b, n), _F32),
        name="ada",
    )(c, w_ada_b, b_ada.reshape(1, n))


def _rotary(t, cos, sin_signed, first_half):
    partner = jnp.where(first_half, pltpu.roll(t, 96, axis=1), pltpu.roll(t, 32, axis=1))
    return t * cos + partner * sin_signed


def _proj_kernel(x_ref, xprev_ref, xnext_ref, mod_ref, g_ref, w_ref, cos_ref, sin_ref, invc_ref,
                 dkbt_ref, gcb_ref, gain_ref, pscale_ref,
                 q_ref, kt_ref, v_ref, gs_ref, pooled_ref, gbs_ref, sb_ref,
                 state_ref, ktf_ref, *, tile):
    n_chunks = tile // CHUNK
    ext = tile + 2 * HALO
    step = pl.program_id(1)
    t_idx = pl.num_programs(1) - 1 - step

    @pl.when(step == 0)
    def _():
        state_ref[...] = jnp.zeros_like(state_ref)

    shift = mod_ref[0:1, :]
    gain = g_ref[...] * (1.0 + mod_ref[1:2, :])
    h = _adaln_rmsnorm(x_ref[...], gain, shift)
    h_prev = jnp.where(t_idx > 0, _adaln_rmsnorm(xprev_ref[...], gain, shift), 0.0)
    h_next = jnp.where(t_idx < pl.num_programs(1) - 1, _adaln_rmsnorm(xnext_ref[...], gain, shift), 0.0)
    hb = h.astype(_BF16)
    hb_ext = jnp.concatenate([h_prev, h, h_next], axis=0).astype(_BF16)

    def proj(off, width):
        return _dot(hb, w_ref[:, off:off + width])

    for j in range(POOL_WIDTH // 256):
        pe2 = _dot(hb_ext, w_ref[:, OFF_P + 256 * j:OFF_P + 256 * (j + 1)])
        for hh in range(2):
            g = 2 * j + hh
            w = POOL_WINDOWS[g]
            pe = pe2[:, 128 * hh:128 * (hh + 1)]
            acc = pe
            span = 1
            while span < w:
                acc = acc + _shift_rows(acc, span)
                span *= 2
            acc = _shift_rows(acc, ext - (w // 2 - 1))[HALO:HALO + tile, :]
            pooled = acc * invc_ref[g] - pe[HALO:HALO + tile, :]
            pooled_ref[:, POOL_GROUP_IN * g:POOL_GROUP_IN * (g + 1)] = pooled.astype(_BF16)

    lane = lax.broadcasted_iota(jnp.int32, (tile, 128), 1)
    first_half = (lane % QK_DIM) < (QK_DIM // 2)
    cos = cos_ref[...]
    sin = sin_ref[...]
    for j in range(RET_QK // 256):
        q2 = proj(OFF_Q + 256 * j, 256)
        k2 = proj(OFF_K + 256 * j, 256)
        for hh in range(2):
            cols = slice(256 * j + 128 * hh, 256 * j + 128 * (hh + 1))
            half = slice(128 * hh, 128 * (hh + 1))
            q_ref[:, cols] = _rotary(q2[:, half], cos, sin, first_half).astype(_BF16)
            kt = (_rotary(k2[:, half], cos, sin, first_half) * (QK_DIM ** -0.5)).T
            ktf_ref[cols, :] = kt
            kt_ref[cols, :] = kt.astype(_BF16)

    for j in range(RET_V // 256):
        cols = slice(256 * j, 256 * (j + 1))
        v_ref[:, cols] = proj(OFF_V + 256 * j, 256).astype(_BF16)
        g = proj(OFF_GSW + 256 * j, 256)
        ga = _sigmoid(proj(OFF_GA + 256 * j, 256))
        gs_ref[:, cols] = (ga * ((g * _sigmoid(g)) * gain_ref[:, cols])).astype(_BF16)
        gb = _sigmoid(proj(OFF_GB + 256 * j, 256))
        gbs_ref[:, cols] = (gb * pscale_ref[:, cols]).astype(_BF16)

    for c in reversed(range(n_chunks)):
        toks = slice(c * CHUNK, (c + 1) * CHUNK)
        for pr in range(N_PAIRS):
            prow = slice(128 * pr, 128 * (pr + 1))
            kdt = (ktf_ref[prow, toks] * dkbt_ref[prow, :]).astype(_BF16)
            upd = _pair_diag(_dot(kdt, v_ref[toks, 256 * pr:256 * (pr + 1)]))
            for hh in range(2):
                hd = 2 * pr + hh
                s_old = state_ref[hd]
                sb_ref[c, hd] = s_old.astype(_BF16)
                state_ref[hd] = s_old * gcb_ref[hd] + upd[hh]


def _proj_call(x, mod, g_mix, w_in_b, cos_t, sin_t, invc, dkbt, gcb, gain, pool_scale, *, tile):
    b, s, _ = x.shape
    nt = s // tile
    nc = s // CHUNK
    cpt = tile // CHUNK

    def tile_of_step(j):
        return nt - 1 - j

    def tok_spec(width):
        return pl.BlockSpec((None, tile, width), lambda bi, j: (bi, tile_of_step(j), 0))

    def kt_spec():
        return pl.BlockSpec((None, RET_QK, tile), lambda bi, j: (bi, 0, tile_of_step(j)))

    def edge_variant(bi, j):
        t = tile_of_step(j)
        return ((t == 0).astype(jnp.int32) + 2 * (t == nt - 1).astype(jnp.int32), 0, 0, 0)

    out_shape = (
        jax.ShapeDtypeStruct((b, s, RET_QK), _BF16),
        jax.ShapeDtypeStruct((b, RET_QK, s), _BF16),
        jax.ShapeDtypeStruct((b, s, RET_V), _BF16),
        jax.ShapeDtypeStruct((b, s, RET_V), _BF16),
        jax.ShapeDtypeStruct((b, s, POOL_WIDTH), _BF16),
        jax.ShapeDtypeStruct((b, s, D_MODEL), _BF16),
        jax.ShapeDtypeStruct((b, nc, N_HEADS, QK_DIM, V_DIM), _BF16),
    )
    out_specs = (
        tok_spec(RET_QK), kt_spec(), tok_spec(RET_V), tok_spec(RET_V),
        tok_spec(POOL_WIDTH), tok_spec(D_MODEL),
        pl.BlockSpec((None, cpt, N_HEADS, QK_DIM, V_DIM),
                     lambda bi, j: (bi, tile_of_step(j), 0, 0, 0)),
    )
    prev_spec, next_spec = _halo_specs(tile, s, tile_of_step)
    in_specs = [
        tok_spec(D_MODEL), prev_spec, next_spec,
        pl.BlockSpec((None, N_MOD, D_MODEL), lambda bi, j: (bi, 0, 0)),
        _const_spec((1, D_MODEL)),
        _const_spec((D_MODEL, IN_WIDTH)),
        pl.BlockSpec((tile, 128), lambda bi, j: (tile_of_step(j), 0)),
        pl.BlockSpec((tile, 128), lambda bi, j: (tile_of_step(j), 0)),
        pl.BlockSpec((None, N_POOL_GROUPS, tile, POOL_GROUP_IN), edge_variant),
        _const_spec((RET_QK, CHUNK)),
        _const_spec((N_HEADS, 1, V_DIM)),
        _const_spec((1, RET_V)),
        _const_spec((1, D_MODEL)),
    ]
    return pl.pallas_call(
        functools.partial(_proj_kernel, tile=tile),
        grid=(b, nt),
        in_specs=in_specs,
        out_specs=out_specs,
        out_shape=out_shape,
        scratch_shapes=[pltpu.VMEM((N_HEADS, QK_DIM, V_DIM), _F32),
                        pltpu.VMEM((RET_QK, tile), _F32)],
        compiler_params=pltpu.CompilerParams(
            dimension_semantics=("arbitrary", "arbitrary"),
            vmem_limit_bytes=VMEM_LIMIT_BYTES),
        name="proj",
    )(x, x, x, mod, g_mix, w_in_b, cos_t, sin_t, invc, dkbt, gcb, gain, pool_scale)


def _mix_kernel(mod_ref, q_ref, kt_ref, v_ref, gs_ref, pooled_ref, gbs_ref, sb_ref,
                dpair_ref, dqf_ref, dqb_ref, dkft_ref, gcf_ref, wpool_ref, wout_ref,
                o_ref, state_ref, mb_ref, merged_ref, *, tile):
    n_chunks = tile // CHUNK

    @pl.when(pl.program_id(1) == 0)
    def _():
        state_ref[...] = jnp.zeros_like(state_ref)

    for g in range(N_POOL_GROUPS):
        ocols = slice(POOL_GROUP_OUT * g, POOL_GROUP_OUT * (g + 1))
        bb = _dot(pooled_ref[:, POOL_GROUP_IN * g:POOL_GROUP_IN * (g + 1)], wpool_ref[g])
        mb_ref[:, ocols] = gbs_ref[:, ocols].astype(_F32) * bb

    zeros_v = jnp.zeros((CHUNK, V_DIM), _BF16)
    zeros_s = jnp.zeros((QK_DIM, V_DIM), _BF16)
    zeros_k = jnp.zeros((QK_DIM, CHUNK), _BF16)

    def pair_state(sa, sb):
        return jnp.concatenate([jnp.concatenate([sa, zeros_s], axis=1),
                                jnp.concatenate([zeros_s, sb], axis=1)], axis=0)

    for c in range(n_chunks):
        toks = slice(c * CHUNK, (c + 1) * CHUNK)
        for pr in range(N_PAIRS):
            cols = slice(128 * pr, 128 * (pr + 1))
            vcols = slice(256 * pr, 256 * (pr + 1))
            ha, hb = 2 * pr, 2 * pr + 1
            qp = q_ref[toks, cols]
            kt = kt_ref[cols, toks]
            vp = v_ref[toks, vcols]
            qf32 = qp.astype(_F32)
            kt_heads = jnp.concatenate([jnp.concatenate([kt[:QK_DIM], zeros_k], axis=0),
                                        jnp.concatenate([zeros_k, kt[QK_DIM:]], axis=0)], axis=1)
            scores = _dot(qp, kt_heads)
            pmat = (scores * dpair_ref[pr]).astype(_BF16)
            qf = (qf32 * dqf_ref[:, cols]).astype(_BF16)
            qb = (qf32 * dqb_ref[:, cols]).astype(_BF16)
            lhs = jnp.concatenate([pmat, qf, qb], axis=1)
            vbd = jnp.concatenate(
                [jnp.concatenate([vp[:, :V_DIM], zeros_v], axis=1),
                 jnp.concatenate([zeros_v, vp[:, V_DIM:]], axis=1)], axis=0)
            sf_a = state_ref[ha]
            sf_b = state_ref[hb]
            rhs = jnp.concatenate([vbd,
                                   pair_state(sf_a.astype(_BF16), sf_b.astype(_BF16)),
                                   pair_state(sb_ref[c, ha], sb_ref[c, hb])], axis=0)
            o = _dot(lhs, rhs)
            kdt = (kt.astype(_F32) * dkft_ref[cols, :]).astype(_BF16)
            upd_a, upd_b = _pair_diag(_dot(kdt, vp))
            state_ref[ha] = sf_a * gcf_ref[ha] + upd_a
            state_ref[hb] = sf_b * gcf_ref[hb] + upd_b
            normed = []
            for hh in range(2):
                oh = o[:, V_DIM * hh:V_DIM * (hh + 1)]
                d = oh - jnp.mean(oh, axis=-1, keepdims=True)
                var = jnp.mean(d * d, axis=-1, keepdims=True)
                normed.append(d * lax.rsqrt(var + EPS))
            merged = (gs_ref[toks, vcols].astype(_F32) * jnp.concatenate(normed, axis=1)
                      + mb_ref[toks, vcols])
            merged_ref[toks, vcols] = merged.astype(_BF16)

    o_ref[...] = mod_ref[2:3, :] * _dot(merged_ref[...], wout_ref[...])


def _mix_call(mod, q, kt, v, gs, pooled, gbs, sb, dpair, dqf, dqb, dkft, gcf,
              w_pool_b, w_out_b, *, tile):
    b, s, _ = q.shape
    nt = s // tile
    cpt = tile // CHUNK

    def tok_spec(width):
        return pl.BlockSpec((None, tile, width), lambda bi, j: (bi, j, 0))

    def kt_spec():
        return pl.BlockSpec((None, RET_QK, tile), lambda bi, j: (bi, 0, j))

    in_specs = [
        pl.BlockSpec((None, N_MOD, D_MODEL), lambda bi, j: (bi, 0, 0)),
        tok_spec(RET_QK), kt_spec(), tok_spec(RET_V), tok_spec(RET_V),
        tok_spec(POOL_WIDTH), tok_spec(D_MODEL),
        pl.BlockSpec((None, cpt, N_HEADS, QK_DIM, V_DIM), lambda bi, j: (bi, j, 0, 0, 0)),
        _const_spec((N_PAIRS, CHUNK, 2 * CHUNK)),
        _const_spec((CHUNK, RET_QK)), _const_spec((CHUNK, RET_QK)), _const_spec((RET_QK, CHUNK)),
        _const_spec((N_HEADS, 1, V_DIM)),
        _const_spec((N_POOL_GROUPS, POOL_GROUP_IN, POOL_GROUP_OUT)),
        _const_spec((D_MODEL, D_MODEL)),
    ]
    return pl.pallas_call(
        functools.partial(_mix_kernel, tile=tile),
        grid=(b, nt),
        in_specs=in_specs,
        out_specs=tok_spec(D_MODEL),
        out_shape=jax.ShapeDtypeStruct((b, s, D_MODEL), _F32),
        scratch_shapes=[pltpu.VMEM((N_HEADS, QK_DIM, V_DIM), _F32),
                        pltpu.VMEM((tile, D_MODEL), _F32),
                        pltpu.VMEM((tile, D_MODEL), _BF16)],
        compiler_params=pltpu.CompilerParams(
            dimension_semantics=("arbitrary", "arbitrary"),
            vmem_limit_bytes=VMEM_LIMIT_BYTES),
        name="mix",
    )(mod, q, kt, v, gs, pooled, gbs, sb, dpair, dqf, dqb, dkft, gcf, w_pool_b, w_out_b)


def _gelu_tanh(x):
    c = math.sqrt(2.0 / math.pi)
    half_x = 0.5 * x
    return half_x + half_x * jnp.tanh(x * (c + (c * 0.044715) * (x * x)))


def _ffn_kernel(x_ref, xprev_ref, xnext_ref, m_ref, mprev_ref, mnext_ref, mod_ref, g_ref, wup_ref,
                cw_ref, cb_ref, wdown_ref, gfin_ref, o_ref, x1_ref, act_head_ref, act_tail_ref, *, tile):
    t_idx = pl.program_id(1)
    ext = tile + 2 * HALO
    shift = mod_ref[3:4, :]
    gain = g_ref[...] * (1.0 + mod_ref[4:5, :])
    gate = mod_ref[5:6, :]

    x1 = x_ref[...] + m_ref[...]
    x1_ref[...] = x1
    h_prev = jnp.where(t_idx > 0,
                       _adaln_rmsnorm(xprev_ref[...] + mprev_ref[...], gain, shift), 0.0)
    h_next = jnp.where(t_idx < pl.num_programs(1) - 1,
                       _adaln_rmsnorm(xnext_ref[...] + mnext_ref[...], gain, shift), 0.0)
    hb = jnp.concatenate([_adaln_rmsnorm(x1, gain, shift), h_next, h_prev], axis=0).astype(_BF16)

    def conv(u, cols):
        out = (_shift_rows(u, 1) * cw_ref[0:1, cols] + u * cw_ref[1:2, cols]
               + _shift_rows(u, ext - 1) * cw_ref[2:3, cols])
        return out[:tile, :] + cb_ref[:, cols]

    n_blocks = D_FF // FF_BLOCK
    n_head = n_blocks - FF_TAIL_BLOCKS
    for j in range(n_blocks):
        ca = slice(FF_BLOCK * j, FF_BLOCK * (j + 1))
        cl = slice(D_FF + FF_BLOCK * j, D_FF + FF_BLOCK * (j + 1))
        ua = _dot(hb, wup_ref[:, ca])
        ul = _dot(hb, wup_ref[:, cl])
        act = (_gelu_tanh(conv(ua, ca)) * conv(ul, cl)).astype(_BF16)
        if j < n_head:
            act_head_ref[:, ca] = act
        else:
            jt = j - n_head
            act_tail_ref[:, FF_BLOCK * jt:FF_BLOCK * (jt + 1)] = act

    k_head = n_head * FF_BLOCK
    ffn = (_dot(act_head_ref[...], wdown_ref[:k_head, :])
           + _dot(act_tail_ref[...], wdown_ref[k_head:, :]))
    y = x1_ref[...] + gate * ffn
    o_ref[...] = y * _rms_scale(y) * gfin_ref[...]


def _ffn_call(x, m, mod, g_ffn, w_up_b, conv_w, conv_b, w_down_b, g_final, *, tile):
    b, s, _ = x.shape
    nt = s // tile
    n_head = D_FF // FF_BLOCK - FF_TAIL_BLOCKS

    tok_spec = pl.BlockSpec((None, tile, D_MODEL), lambda bi, j: (bi, j, 0))
    prev_spec, next_spec = _halo_specs(tile, s, lambda j: j)
    in_specs = [
        tok_spec, prev_spec, next_spec, tok_spec, prev_spec, next_spec,
        pl.BlockSpec((None, N_MOD, D_MODEL), lambda bi, j: (bi, 0, 0)),
        _const_spec((1, D_MODEL)),
        _const_spec((D_MODEL, 2 * D_FF)),
        _const_spec((3, 2 * D_FF)),
        _const_spec((1, 2 * D_FF)),
        _const_spec((D_FF, D_MODEL)),
        _const_spec((1, D_MODEL)),
    ]
    return pl.pallas_call(
        functools.partial(_ffn_kernel, tile=tile),
        grid=(b, nt),
        in_specs=in_specs,
        out_specs=tok_spec,
        out_shape=jax.ShapeDtypeStruct((b, s, D_MODEL), _F32),
        scratch_shapes=[pltpu.VMEM((tile, D_MODEL), _F32),
                        pltpu.VMEM((tile, n_head * FF_BLOCK), _BF16),
                        pltpu.VMEM((tile, FF_TAIL_BLOCKS * FF_BLOCK), _BF16)],
        compiler_params=pltpu.CompilerParams(
            dimension_semantics=("arbitrary", "arbitrary"),
            vmem_limit_bytes=VMEM_LIMIT_BYTES),
        name="ffn",
    )(x, x, x, m, m, m, mod, g_ffn, w_up_b, conv_w, conv_b, w_down_b, g_final)


def _rope_tables(s):
    half = QK_DIM // 2
    inv_freq = ROPE_BASE ** (-jnp.arange(half, dtype=_F32) / half)
    ang = jnp.arange(s, dtype=_F32)[:, None] * inv_freq[None, :]
    cos = jnp.tile(jnp.cos(ang), (1, 4))
    sin = jnp.tile(jnp.sin(ang), (1, 4))
    first_half = (jnp.arange(128) % QK_DIM) < half
    return cos, jnp.where(first_half[None, :], -sin, sin)


def _pool_inv_counts(tile):
    row = jnp.arange(tile)
    variants = []
    for v in range(4):
        per_group = []
        for w in POOL_WINDOWS:
            lo = row - w // 2
            hi = row + w // 2
            if v & 1:
                lo = jnp.maximum(lo, 0)
            if v & 2:
                hi = jnp.minimum(hi, tile)
            inv = 1.0 / (hi - lo).astype(_F32)
            per_group.append(jnp.broadcast_to(inv[:, None], (tile, POOL_GROUP_IN)))
        variants.append(jnp.stack(per_group))
    return jnp.stack(variants)


def _decay_tables(ret_decay_f, ret_decay_b):
    lg_f = jax.nn.log_sigmoid(ret_decay_f.astype(_F32))
    lg_b = jax.nn.log_sigmoid(ret_decay_b.astype(_F32))
    idx = jnp.arange(CHUNK, dtype=_F32)
    diff = idx[:, None] - idx[None, :]
    causal = diff >= 0.0
    d_f = jnp.exp(jnp.where(causal, diff, 0.0)[None] * lg_f[:, None, None])
    d_b = jnp.exp(jnp.where(causal, 0.0, -diff)[None] * lg_b[:, None, None])
    d = jnp.where(causal[None], d_f, d_b)
    dpair = d.reshape(N_PAIRS, 2, CHUNK, CHUNK).transpose(0, 2, 1, 3).reshape(N_PAIRS, CHUNK, 2 * CHUNK)

    def per_lane(power, lg):
        return jnp.repeat(jnp.exp(power[:, None] * lg[None, :]), QK_DIM, axis=1)

    def chunk_decay(lg):
        return jnp.broadcast_to(jnp.exp(CHUNK * lg)[:, None, None], (N_HEADS, 1, V_DIM))

    return dict(dpair=dpair, dqf=per_lane(idx + 1.0, lg_f), dqb=per_lane(CHUNK - idx, lg_b),
                dkft=per_lane(CHUNK - 1.0 - idx, lg_f).T, dkbt=per_lane(idx, lg_b).T,
                gcf=chunk_decay(lg_f), gcb=chunk_decay(lg_b))


def _encoder(x, c, w, *, tile):
    b, s, _ = x.shape
    mod = _ada_call(c, w["w_ada"], w["b_ada"]).reshape(b, N_MOD, D_MODEL)
    cos_t, sin_t = _rope_tables(s)
    q, kt, v, gs, pooled, gbs, sb = _proj_call(
        x, mod, w["g_mix"], w["w_in"], cos_t, sin_t, _pool_inv_counts(tile), w["dkbt"], w["gcb"],
        w["gain"], w["pool_scale"], tile=tile)
    m = _mix_call(mod, q, kt, v, gs, pooled, gbs, sb, w["dpair"], w["dqf"], w["dqb"],
                  w["dkft"], w["gcf"], w["w_pool"], w["w_out"], tile=tile)
    return _ffn_call(x, m, mod, w["g_ffn"], w["w_up"], w["conv_w"], w["conv_b"], w["w_down"],
                     w["g_final"], tile=tile)


def _prepare_weights(w_ada, b_ada, g_mix, w_in, ret_decay_f, ret_decay_b, ret_gn_gain, w_pool_grp,
                     pool_scale, w_out, g_ffn, w_up, conv_w, conv_b, w_down, g_final):
    return dict(
        _decay_tables(ret_decay_f[0], ret_decay_b[0]),
        w_ada=w_ada[0].astype(_BF16), b_ada=b_ada[0],
        g_mix=g_mix[0].reshape(1, D_MODEL), w_in=w_in[0].astype(_BF16),
        gain=ret_gn_gain[0].reshape(1, RET_V), w_pool=w_pool_grp[0].astype(_BF16),
        pool_scale=pool_scale[0].reshape(1, D_MODEL), w_out=w_out[0].astype(_BF16),
        g_ffn=g_ffn[0].reshape(1, D_MODEL), w_up=w_up[0].astype(_BF16),
        conv_w=conv_w[0], conv_b=conv_b[0].reshape(1, 2 * D_FF),
        w_down=w_down[0].astype(_BF16), g_final=g_final.reshape(1, D_MODEL))


def kernel(x_prompt, x_sample, c_prompt, c_sample, w_ada, b_ada, g_mix, w_in, ret_decay_f,
           ret_decay_b, ret_gn_gain, w_pool_grp, pool_scale, w_out, g_ffn, w_up, conv_w, conv_b,
           w_down, g_final):
    assert w_ada.shape[0] == 1, "single-layer trunk"
    w = _prepare_weights(w_ada, b_ada, g_mix, w_in, ret_decay_f, ret_decay_b, ret_gn_gain,
                         w_pool_grp, pool_scale, w_out, g_ffn, w_up, conv_w, conv_b, w_down, g_final)
    y_prompt = _encoder(x_prompt, c_prompt, w, tile=min(512, x_prompt.shape[1]))
    y_sample = _encoder(x_sample, c_sample, w, tile=min(512, x_sample.shape[1]))
    return (y_prompt, y_sample)
```

```python
import functools
import math

import jax
import jax.numpy as jnp
from jax import lax
from jax.experimental import pallas as pl
from jax.experimental.pallas import tpu as pltpu

D_MODEL = 1024
N_HEADS = 8
QK_DIM = 64
V_DIM = 128
RET_QK = N_HEADS * QK_DIM
RET_V = N_HEADS * V_DIM
N_PAIRS = N_HEADS // 2
CHUNK = 128
ROPE_BASE = 10000.0
POOL_WINDOWS = (2, 4, 8, 16)
N_POOL_GROUPS = 4
POOL_WIDTH = 512
POOL_GROUP_IN = 128
POOL_GROUP_OUT = 256
D_FF = 2816
N_MOD = 6
EPS = 1e-6
IN_WIDTH = 2 * RET_QK + 2 * RET_V + POOL_WIDTH + 2 * D_MODEL
OFF_Q, OFF_K, OFF_V, OFF_GSW, OFF_P, OFF_GA, OFF_GB = 0, 512, 1024, 2048, 3072, 3584, 4608

HALO = 8
FF_BLOCK = 256
FF_TAIL_BLOCKS = 2
BF16_ROWS = 16
VMEM_LIMIT_BYTES = 56 * 1024 * 1024

_F32 = jnp.float32
_BF16 = jnp.bfloat16


def _const_spec(shape):
    nd = len(shape)
    return pl.BlockSpec(shape, lambda *_: (0,) * nd, pipeline_mode=pl.Buffered(1))


def _sigmoid(x):
    return 1.0 / (1.0 + jnp.exp(-x))


def _rms_scale(x):
    return lax.rsqrt(jnp.mean(x * x, axis=-1, keepdims=True) + EPS)


def _adaln_rmsnorm(v, gain, shift):
    return (v * _rms_scale(v)) * gain + shift


def _dot(a, b):
    return jnp.dot(a, b, preferred_element_type=_F32)


def _dot_two_row_parts(a, b):
    split = (a.shape[0] // (2 * BF16_ROWS)) * BF16_ROWS
    return jnp.concatenate([_dot(a[:split], b), _dot(a[split:], b)], axis=0)


def _pair_diag(upd):
    return upd[:QK_DIM, :V_DIM], upd[QK_DIM:, V_DIM:]


def _shift_rows(x, k):
    k = k % x.shape[0]
    return x if k == 0 else pltpu.roll(x, k, axis=0)


def _halo_specs(tile, seq_len, tile_of_step):
    hpt = tile // HALO
    n_halo = seq_len // HALO
    prev = pl.BlockSpec((None, HALO, D_MODEL),
                        lambda bi, j: (bi, jnp.maximum(tile_of_step(j) * hpt - 1, 0), 0))
    nxt = pl.BlockSpec((None, HALO, D_MODEL),
                       lambda bi, j: (bi, jnp.minimum((tile_of_step(j) + 1) * hpt, n_halo - 1), 0))
    return prev, nxt


def _ada_kernel(c_ref, w_ref, b_ref, o_ref):
    c = c_ref[...]
    a = (c * _sigmoid(c)).astype(_BF16)
    o_ref[...] = _dot(a, w_ref[...]) + b_ref[...]


def _ada_call(c, w_ada_b, b_ada):
    b = c.shape[0]
    n = w_ada_b.shape[1]
    nb = D_MODEL
    return pl.pallas_call(
        _ada_kernel,
        grid=(n // nb,),
        in_specs=[pl.BlockSpec((b, D_MODEL), lambda j: (0, 0)),
                  pl.BlockSpec((D_MODEL, nb), lambda j: (0, j)),
                  pl.BlockSpec((1, nb), lambda j: (0, j))],
        out_specs=pl.BlockSpec((b, nb), lambda j: (0, j)),
        out_shape=jax.ShapeDtypeStruct((b, n), _F32),
        name="ada",
    )(c, w_ada_b, b_ada.reshape(1, n))


def _rotary(t, cos, sin_signed, first_half):
    partner = jnp.where(first_half, pltpu.roll(t, 96, axis=1), pltpu.roll(t, 32, axis=1))
    return t * cos + partner * sin_signed


def _proj_kernel(x_ref, xprev_ref, xnext_ref, mod_ref, g_ref, w_ref, cos_ref, sin_ref, invc_ref,
                 dkbt_ref, gcb_ref, gain_ref, pscale_ref,
                 q_ref, kt_ref, v_ref, gs_ref, pooled_ref, gbs_ref, sb_ref,
                 state_ref, ktf_ref, *, tile):
    n_chunks = tile // CHUNK
    ext = tile + 2 * HALO
    step = pl.program_id(1)
    t_idx = pl.num_programs(1) - 1 - step

    @pl.when(step == 0)
    def _():
        state_ref[...] = jnp.zeros_like(state_ref)

    shift = mod_ref[0:1, :]
    gain = g_ref[...] * (1.0 + mod_ref[1:2, :])
    h = _adaln_rmsnorm(x_ref[...], gain, shift)
    h_prev = jnp.where(t_idx > 0, _adaln_rmsnorm(xprev_ref[...], gain, shift), 0.0)
    h_next = jnp.where(t_idx < pl.num_programs(1) - 1, _adaln_rmsnorm(xnext_ref[...], gain, shift), 0.0)
    hb = h.astype(_BF16)
    hb_ext = jnp.concatenate([h_prev, h, h_next], axis=0).astype(_BF16)

    def proj(off, width):
        return _dot_two_row_parts(hb, w_ref[:, off:off + width])

    for j in range(POOL_WIDTH // 256):
        pe2 = _dot_two_row_parts(hb_ext, w_ref[:, OFF_P + 256 * j:OFF_P + 256 * (j + 1)])
        for hh in range(2):
            g = 2 * j + hh
            w = POOL_WINDOWS[g]
            pe = pe2[:, 128 * hh:128 * (hh + 1)]
            acc = pe
            span = 1
            while span < w:
                acc = acc + _shift_rows(acc, span)
                span *= 2
            acc = _shift_rows(acc, ext - (w // 2 - 1))[HALO:HALO + tile, :]
            pooled = acc * invc_ref[g] - pe[HALO:HALO + tile, :]
            pooled_ref[:, POOL_GROUP_IN * g:POOL_GROUP_IN * (g + 1)] = pooled.astype(_BF16)

    lane = lax.broadcasted_iota(jnp.int32, (tile, 128), 1)
    first_half = (lane % QK_DIM) < (QK_DIM // 2)
    cos = cos_ref[...]
    sin = sin_ref[...]
    for j in range(RET_QK // 256):
        q2 = proj(OFF_Q + 256 * j, 256)
        k2 = proj(OFF_K + 256 * j, 256)
        for hh in range(2):
            cols = slice(256 * j + 128 * hh, 256 * j + 128 * (hh + 1))
            half = slice(128 * hh, 128 * (hh + 1))
            q_ref[:, cols] = _rotary(q2[:, half], cos, sin, first_half).astype(_BF16)
            kt = (_rotary(k2[:, half], cos, sin, first_half) * (QK_DIM ** -0.5)).T
            ktf_ref[cols, :] = kt
            kt_ref[cols, :] = kt.astype(_BF16)

    for j in range(RET_V // 256):
        cols = slice(256 * j, 256 * (j + 1))
        v_ref[:, cols] = proj(OFF_V + 256 * j, 256).astype(_BF16)
        g = proj(OFF_GSW + 256 * j, 256)
        ga = _sigmoid(proj(OFF_GA + 256 * j, 256))
        gs_ref[:, cols] = (ga * ((g * _sigmoid(g)) * gain_ref[:, cols])).astype(_BF16)
        gb = _sigmoid(proj(OFF_GB + 256 * j, 256))
        gbs_ref[:, cols] = (gb * pscale_ref[:, cols]).astype(_BF16)

    for c in reversed(range(n_chunks)):
        toks = slice(c * CHUNK, (c + 1) * CHUNK)
        for pr in range(N_PAIRS):
            prow = slice(128 * pr, 128 * (pr + 1))
            kdt = (ktf_ref[prow, toks] * dkbt_ref[prow, :]).astype(_BF16)
            upd = _pair_diag(_dot(kdt, v_ref[toks, 256 * pr:256 * (pr + 1)]))
            for hh in range(2):
                hd = 2 * pr + hh
                s_old = state_ref[hd]
                sb_ref[c, hd] = s_old.astype(_BF16)
                state_ref[hd] = s_old * gcb_ref[hd] + upd[hh]


def _proj_call(x, mod, g_mix, w_in_b, cos_t, sin_t, invc, dkbt, gcb, gain, pool_scale, *, tile):
    b, s, _ = x.shape
    nt = s // tile
    nc = s // CHUNK
    cpt = tile // CHUNK

    def tile_of_step(j):
        return nt - 1 - j

    def tok_spec(width):
        return pl.BlockSpec((None, tile, width), lambda bi, j: (bi, tile_of_step(j), 0))

    def kt_spec():
        return pl.BlockSpec((None, RET_QK, tile), lambda bi, j: (bi, 0, tile_of_step(j)))

    def edge_variant(bi, j):
        t = tile_of_step(j)
        return ((t == 0).astype(jnp.int32) + 2 * (t == nt - 1).astype(jnp.int32), 0, 0, 0)

    out_shape = (
        jax.ShapeDtypeStruct((b, s, RET_QK), _BF16),
        jax.ShapeDtypeStruct((b, RET_QK, s), _BF16),
        jax.ShapeDtypeStruct((b, s, RET_V), _BF16),
        jax.ShapeDtypeStruct((b, s, RET_V), _BF16),
        jax.ShapeDtypeStruct((b, s, POOL_WIDTH), _BF16),
        jax.ShapeDtypeStruct((b, s, D_MODEL), _BF16),
        jax.ShapeDtypeStruct((b, nc, N_HEADS, QK_DIM, V_DIM), _BF16),
    )
    out_specs = (
        tok_spec(RET_QK), kt_spec(), tok_spec(RET_V), tok_spec(RET_V),
        tok_spec(POOL_WIDTH), tok_spec(D_MODEL),
        pl.BlockSpec((None, cpt, N_HEADS, QK_DIM, V_DIM),
                     lambda bi, j: (bi, tile_of_step(j), 0, 0, 0)),
    )
    prev_spec, next_spec = _halo_specs(tile, s, tile_of_step)
    in_specs = [
        tok_spec(D_MODEL), prev_spec, next_spec,
        pl.BlockSpec((None, N_MOD, D_MODEL), lambda bi, j: (bi, 0, 0)),
        _const_spec((1, D_MODEL)),
        _const_spec((D_MODEL, IN_WIDTH)),
        pl.BlockSpec((tile, 128), lambda bi, j: (tile_of_step(j), 0)),
        pl.BlockSpec((tile, 128), lambda bi, j: (tile_of_step(j), 0)),
        pl.BlockSpec((None, N_POOL_GROUPS, tile, POOL_GROUP_IN), edge_variant),
        _const_spec((RET_QK, CHUNK)),
        _const_spec((N_HEADS, 1, V_DIM)),
        _const_spec((1, RET_V)),
        _const_spec((1, D_MODEL)),
    ]
    return pl.pallas_call(
        functools.partial(_proj_kernel, tile=tile),
        grid=(b, nt),
        in_specs=in_specs,
        out_specs=out_specs,
        out_shape=out_shape,
        scratch_shapes=[pltpu.VMEM((N_HEADS, QK_DIM, V_DIM), _F32),
                        pltpu.VMEM((RET_QK, tile), _F32)],
        compiler_params=pltpu.CompilerParams(
            dimension_semantics=("arbitrary", "arbitrary"),
            vmem_limit_bytes=VMEM_LIMIT_BYTES),
        name="proj",
    )(x, x, x, mod, g_mix, w_in_b, cos_t, sin_t, invc, dkbt, gcb, gain, pool_scale)


def _mix_kernel(mod_ref, q_ref, kt_ref, v_ref, gs_ref, pooled_ref, gbs_ref, sb_ref,
                dpair_ref, dqf_ref, dqb_ref, dkft_ref, gcf_ref, wpool_ref, wout_ref,
                o_ref, state_ref, mb_ref, merged_ref, *, tile):
    n_chunks = tile // CHUNK

    @pl.when(pl.program_id(1) == 0)
    def _():
        state_ref[...] = jnp.zeros_like(state_ref)

    for g in range(N_POOL_GROUPS):
        ocols = slice(POOL_GROUP_OUT * g, POOL_GROUP_OUT * (g + 1))
        bb = _dot(pooled_ref[:, POOL_GROUP_IN * g:POOL_GROUP_IN * (g + 1)], wpool_ref[g])
        mb_ref[:, ocols] = gbs_ref[:, ocols].astype(_F32) * bb

    zeros_v = jnp.zeros((CHUNK, V_DIM), _BF16)
    zeros_s = jnp.zeros((QK_DIM, V_DIM), _BF16)
    zeros_k = jnp.zeros((QK_DIM, CHUNK), _BF16)

    def pair_state(sa, sb):
        return jnp.concatenate([jnp.concatenate([sa, zeros_s], axis=1),
                                jnp.concatenate([zeros_s, sb], axis=1)], axis=0)

    for c in range(n_chunks):
        toks = slice(c * CHUNK, (c + 1) * CHUNK)
        for pr in range(N_PAIRS):
            cols = slice(128 * pr, 128 * (pr + 1))
            vcols = slice(256 * pr, 256 * (pr + 1))
            ha, hb = 2 * pr, 2 * pr + 1
            qp = q_ref[toks, cols]
            kt = kt_ref[cols, toks]
            vp = v_ref[toks, vcols]
            qf32 = qp.astype(_F32)
            kt_heads = jnp.concatenate([jnp.concatenate([kt[:QK_DIM], zeros_k], axis=0),
                                        jnp.concatenate([zeros_k, kt[QK_DIM:]], axis=0)], axis=1)
            scores = _dot(qp, kt_heads)
            pmat = (scores * dpair_ref[pr]).astype(_BF16)
            qf = (qf32 * dqf_ref[:, cols]).astype(_BF16)
            qb = (qf32 * dqb_ref[:, cols]).astype(_BF16)
            lhs = jnp.concatenate([pmat, qf, qb], axis=1)
            vbd = jnp.concatenate(
                [jnp.concatenate([vp[:, :V_DIM], zeros_v], axis=1),
                 jnp.concatenate([zeros_v, vp[:, V_DIM:]], axis=1)], axis=0)
            sf_a = state_ref[ha]
            sf_b = state_ref[hb]
            rhs = jnp.concatenate([vbd,
                                   pair_state(sf_a.astype(_BF16), sf_b.astype(_BF16)),
                                   pair_state(sb_ref[c, ha], sb_ref[c, hb])], axis=0)
            o = _dot(lhs, rhs)
            kdt = (kt.astype(_F32) * dkft_ref[cols, :]).astype(_BF16)
            upd_a, upd_b = _pair_diag(_dot(kdt, vp))
            state_ref[ha] = sf_a * gcf_ref[ha] + upd_a
            state_ref[hb] = sf_b * gcf_ref[hb] + upd_b
            normed = []
            for hh in range(2):
                oh = o[:, V_DIM * hh:V_DIM * (hh + 1)]
                d = oh - jnp.mean(oh, axis=-1, keepdims=True)
                var = jnp.mean(d * d, axis=-1, keepdims=True)
                normed.append(d * lax.rsqrt(var + EPS))
            merged = (gs_ref[toks, vcols].astype(_F32) * jnp.concatenate(normed, axis=1)
                      + mb_ref[toks, vcols])
            merged_ref[toks, vcols] = merged.astype(_BF16)

    o_ref[...] = mod_ref[2:3, :] * _dot(merged_ref[...], wout_ref[...])


def _mix_call(mod, q, kt, v, gs, pooled, gbs, sb, dpair, dqf, dqb, dkft, gcf,
              w_pool_b, w_out_b, *, tile):
    b, s, _ = q.shape
    nt = s // tile
    cpt = tile // CHUNK

    def tok_spec(width):
        return pl.BlockSpec((None, tile, width), lambda bi, j: (bi, j, 0))

    def kt_spec():
        return pl.BlockSpec((None, RET_QK, tile), lambda bi, j: (bi, 0, j))

    in_specs = [
        pl.BlockSpec((None, N_MOD, D_MODEL), lambda bi, j: (bi, 0, 0)),
        tok_spec(RET_QK), kt_spec(), tok_spec(RET_V), tok_spec(RET_V),
        tok_spec(POOL_WIDTH), tok_spec(D_MODEL),
        pl.BlockSpec((None, cpt, N_HEADS, QK_DIM, V_DIM), lambda bi, j: (bi, j, 0, 0, 0)),
        _const_spec((N_PAIRS, CHUNK, 2 * CHUNK)),
        _const_spec((CHUNK, RET_QK)), _const_spec((CHUNK, RET_QK)), _const_spec((RET_QK, CHUNK)),
        _const_spec((N_HEADS, 1, V_DIM)),
        _const_spec((N_POOL_GROUPS, POOL_GROUP_IN, POOL_GROUP_OUT)),
        _const_spec((D_MODEL, D_MODEL)),
    ]
    return pl.pallas_call(
        functools.partial(_mix_kernel, tile=tile),
        grid=(b, nt),
        in_specs=in_specs,
        out_specs=tok_spec(D_MODEL),
        out_shape=jax.ShapeDtypeStruct((b, s, D_MODEL), _F32),
        scratch_shapes=[pltpu.VMEM((N_HEADS, QK_DIM, V_DIM), _F32),
                        pltpu.VMEM((tile, D_MODEL), _F32),
                        pltpu.VMEM((tile, D_MODEL), _BF16)],
        compiler_params=pltpu.CompilerParams(
            dimension_semantics=("arbitrary", "arbitrary"),
            vmem_limit_bytes=VMEM_LIMIT_BYTES),
        name="mix",
    )(mod, q, kt, v, gs, pooled, gbs, sb, dpair, dqf, dqb, dkft, gcf, w_pool_b, w_out_b)


def _gelu_tanh(x):
    c = math.sqrt(2.0 / math.pi)
    half_x = 0.5 * x
    return half_x + half_x * jnp.tanh(x * (c + (c * 0.044715) * (x * x)))


def _ffn_kernel(x_ref, xprev_ref, xnext_ref, m_ref, mprev_ref, mnext_ref, mod_ref, g_ref, wup_ref,
                cw_ref, cb_ref, wdown_ref, gfin_ref, o_ref, x1_ref, act_head_ref, act_tail_ref, *, tile):
    t_idx = pl.program_id(1)
    ext = tile + 2 * HALO
    shift = mod_ref[3:4, :]
    gain = g_ref[...] * (1.0 + mod_ref[4:5, :])
    gate = mod_ref[5:6, :]

    x1 = x_ref[...] + m_ref[...]
    x1_ref[...] = x1
    h_prev = jnp.where(t_idx > 0,
                       _adaln_rmsnorm(xprev_ref[...] + mprev_ref[...], gain, shift), 0.0)
    h_next = jnp.where(t_idx < pl.num_programs(1) - 1,
                       _adaln_rmsnorm(xnext_ref[...] + mnext_ref[...], gain, shift), 0.0)
    hb = jnp.concatenate([_adaln_rmsnorm(x1, gain, shift), h_next, h_prev], axis=0).astype(_BF16)

    def conv(u, cols):
        out = (_shift_rows(u, 1) * cw_ref[0:1, cols] + u * cw_ref[1:2, cols]
               + _shift_rows(u, ext - 1) * cw_ref[2:3, cols])
        return out[:tile, :] + cb_ref[:, cols]

    def up(cols):
        return _dot_two_row_parts(hb, wup_ref[:, cols])

    n_blocks = D_FF // FF_BLOCK
    n_head = n_blocks - FF_TAIL_BLOCKS
    for j in range(n_blocks):
        ca = slice(FF_BLOCK * j, FF_BLOCK * (j + 1))
        cl = slice(D_FF + FF_BLOCK * j, D_FF + FF_BLOCK * (j + 1))
        ua = up(ca)
        ul = up(cl)
        act = (_gelu_tanh(conv(ua, ca)) * conv(ul, cl)).astype(_BF16)
        if j < n_head:
            act_head_ref[:, ca] = act
        else:
            jt = j - n_head
            act_tail_ref[:, FF_BLOCK * jt:FF_BLOCK * (jt + 1)] = act

    k_head = n_head * FF_BLOCK
    ffn = (_dot(act_head_ref[...], wdown_ref[:k_head, :])
           + _dot(act_tail_ref[...], wdown_ref[k_head:, :]))
    y = x1_ref[...] + gate * ffn
    o_ref[...] = y * _rms_scale(y) * gfin_ref[...]


def _ffn_call(x, m, mod, g_ffn, w_up_b, conv_w, conv_b, w_down_b, g_final, *, tile):
    b, s, _ = x.shape
    nt = s // tile
    n_head = D_FF // FF_BLOCK - FF_TAIL_BLOCKS

    tok_spec = pl.BlockSpec((None, tile, D_MODEL), lambda bi, j: (bi, j, 0))
    prev_spec, next_spec = _halo_specs(tile, s, lambda j: j)
    in_specs = [
        tok_spec, prev_spec, next_spec, tok_spec, prev_spec, next_spec,
        pl.BlockSpec((None, N_MOD, D_MODEL), lambda bi, j: (bi, 0, 0)),
        _const_spec((1, D_MODEL)),
        _const_spec((D_MODEL, 2 * D_FF)),
        _const_spec((3, 2 * D_FF)),
        _const_spec((1, 2 * D_FF)),
        _const_spec((D_FF, D_MODEL)),
        _const_spec((1, D_MODEL)),
    ]
    return pl.pallas_call(
        functools.partial(_ffn_kernel, tile=tile),
        grid=(b, nt),
        in_specs=in_specs,
        out_specs=tok_spec,
        out_shape=jax.ShapeDtypeStruct((b, s, D_MODEL), _F32),
        scratch_shapes=[pltpu.VMEM((tile, D_MODEL), _F32),
                        pltpu.VMEM((tile, n_head * FF_BLOCK), _BF16),
                        pltpu.VMEM((tile, FF_TAIL_BLOCKS * FF_BLOCK), _BF16)],
        compiler_params=pltpu.CompilerParams(
            dimension_semantics=("arbitrary", "arbitrary"),
            vmem_limit_bytes=VMEM_LIMIT_BYTES),
        name="ffn",
    )(x, x, x, m, m, m, mod, g_ffn, w_up_b, conv_w, conv_b, w_down_b, g_final)


def _rope_tables(s):
    half = QK_DIM // 2
    inv_freq = ROPE_BASE ** (-jnp.arange(half, dtype=_F32) / half)
    ang = jnp.arange(s, dtype=_F32)[:, None] * inv_freq[None, :]
    cos = jnp.tile(jnp.cos(ang), (1, 4))
    sin = jnp.tile(jnp.sin(ang), (1, 4))
    first_half = (jnp.arange(128) % QK_DIM) < half
    return cos, jnp.where(first_half[None, :], -sin, sin)


def _pool_inv_counts(tile):
    row = jnp.arange(tile)
    variants = []
    for v in range(4):
        per_group = []
        for w in POOL_WINDOWS:
            lo = row - w // 2
            hi = row + w // 2
            if v & 1:
                lo = jnp.maximum(lo, 0)
            if v & 2:
                hi = jnp.minimum(hi, tile)
            inv = 1.0 / (hi - lo).astype(_F32)
            per_group.append(jnp.broadcast_to(inv[:, None], (tile, POOL_GROUP_IN)))
        variants.append(jnp.stack(per_group))
    return jnp.stack(variants)


def _decay_tables(ret_decay_f, ret_decay_b):
    lg_f = jax.nn.log_sigmoid(ret_decay_f.astype(_F32))
    lg_b = jax.nn.log_sigmoid(ret_decay_b.astype(_F32))
    idx = jnp.arange(CHUNK, dtype=_F32)
    diff = idx[:, None] - idx[None, :]
    causal = diff >= 0.0
    d_f = jnp.exp(jnp.where(causal, diff, 0.0)[None] * lg_f[:, None, None])
    d_b = jnp.exp(jnp.where(causal, 0.0, -diff)[None] * lg_b[:, None, None])
    d = jnp.where(causal[None], d_f, d_b)
    dpair = d.reshape(N_PAIRS, 2, CHUNK, CHUNK).transpose(0, 2, 1, 3).reshape(N_PAIRS, CHUNK, 2 * CHUNK)

    def per_lane(power, lg):
        return jnp.repeat(jnp.exp(power[:, None] * lg[None, :]), QK_DIM, axis=1)

    def chunk_decay(lg):
        return jnp.broadcast_to(jnp.exp(CHUNK * lg)[:, None, None], (N_HEADS, 1, V_DIM))

    return dict(dpair=dpair, dqf=per_lane(idx + 1.0, lg_f), dqb=per_lane(CHUNK - idx, lg_b),
                dkft=per_lane(CHUNK - 1.0 - idx, lg_f).T, dkbt=per_lane(idx, lg_b).T,
                gcf=chunk_decay(lg_f), gcb=chunk_decay(lg_b))


def _encoder(x, c, w, *, tile):
    b, s, _ = x.shape
    mod = _ada_call(c, w["w_ada"], w["b_ada"]).reshape(b, N_MOD, D_MODEL)
    cos_t, sin_t = _rope_tables(s)
    q, kt, v, gs, pooled, gbs, sb = _proj_call(
        x, mod, w["g_mix"], w["w_in"], cos_t, sin_t, _pool_inv_counts(tile), w["dkbt"], w["gcb"],
        w["gain"], w["pool_scale"], tile=tile)
    m = _mix_call(mod, q, kt, v, gs, pooled, gbs, sb, w["dpair"], w["dqf"], w["dqb"],
                  w["dkft"], w["gcf"], w["w_pool"], w["w_out"], tile=tile)
    return _ffn_call(x, m, mod, w["g_ffn"], w["w_up"], w["conv_w"], w["conv_b"], w["w_down"],
                     w["g_final"], tile=tile)


def _prepare_weights(w_ada, b_ada, g_mix, w_in, ret_decay_f, ret_decay_b, ret_gn_gain, w_pool_grp,
                     pool_scale, w_out, g_ffn, w_up, conv_w, conv_b, w_down, g_final):
    return dict(
        _decay_tables(ret_decay_f[0], ret_decay_b[0]),
        w_ada=w_ada[0].astype(_BF16), b_ada=b_ada[0],
        g_mix=g_mix[0].reshape(1, D_MODEL), w_in=w_in[0].astype(_BF16),
        gain=ret_gn_gain[0].reshape(1, RET_V), w_pool=w_pool_grp[0].astype(_BF16),
        pool_scale=pool_scale[0].reshape(1, D_MODEL), w_out=w_out[0].astype(_BF16),
        g_ffn=g_ffn[0].reshape(1, D_MODEL), w_up=w_up[0].astype(_BF16),
        conv_w=conv_w[0], conv_b=conv_b[0].reshape(1, 2 * D_FF),
        w_down=w_down[0].astype(_BF16), g_final=g_final.reshape(1, D_MODEL))


def kernel(x_prompt, x_sample, c_prompt, c_sample, w_ada, b_ada, g_mix, w_in, ret_decay_f,
           ret_decay_b, ret_gn_gain, w_pool_grp, pool_scale, w_out, g_ffn, w_up, conv_w, conv_b,
           w_down, g_final):
    assert w_ada.shape[0] == 1, "single-layer trunk"
    w = _prepare_weights(w_ada, b_ada, g_mix, w_in, ret_decay_f, ret_decay_b, ret_gn_gain,
                         w_pool_grp, pool_scale, w_out, g_ffn, w_up, conv_w, conv_b, w_down, g_final)
    y_prompt = _encoder(x_prompt, c_prompt, w, tile=min(512, x_prompt.shape[1]))
    y_sample = _encoder(x_sample, c_sample, w, tile=min(512, x_sample.shape[1]))
    return (y_prompt, y_sample)
```

```python
import functools
import math

import jax
import jax.numpy as jnp
from jax import lax
from jax.experimental import pallas as pl
from jax.experimental.pallas import tpu as pltpu

D_MODEL = 1024
N_HEADS = 8
QK_DIM = 64
V_DIM = 128
RET_QK = N_HEADS * QK_DIM
RET_V = N_HEADS * V_DIM
N_PAIRS = N_HEADS // 2
CHUNK = 128
ROPE_BASE = 10000.0
POOL_WINDOWS = (2, 4, 8, 16)
N_POOL_GROUPS = 4
POOL_WIDTH = 512
POOL_GROUP_IN = 128
POOL_GROUP_OUT = 256
D_FF = 2816
N_MOD = 6
EPS = 1e-6
IN_WIDTH = 2 * RET_QK + 2 * RET_V + POOL_WIDTH + 2 * D_MODEL
OFF_Q, OFF_K, OFF_V, OFF_GSW, OFF_P, OFF_GA, OFF_GB = 0, 512, 1024, 2048, 3072, 3584, 4608

HALO = 8
FF_BLOCK = 256
FF_TAIL_BLOCKS = 2
BF16_ROWS = 16
VMEM_LIMIT_BYTES = 56 * 1024 * 1024

_F32 = jnp.float32
_BF16 = jnp.bfloat16


def _const_spec(shape):
    nd = len(shape)
    return pl.BlockSpec(shape, lambda *_: (0,) * nd, pipeline_mode=pl.Buffered(1))


def _sigmoid(x):
    return 1.0 / (1.0 + jnp.exp(-x))


def _rms_scale(x):
    return lax.rsqrt(jnp.mean(x * x, axis=-1, keepdims=True) + EPS)


def _adaln_rmsnorm(v, gain, shift):
    return (v * _rms_scale(v)) * gain + shift


def _dot(a, b):
    return jnp.dot(a, b, preferred_element_type=_F32)


def _dot_two_row_parts(a, b):
    split = (a.shape[0] // (2 * BF16_ROWS)) * BF16_ROWS
    return jnp.concatenate([_dot(a[:split], b), _dot(a[split:], b)], axis=0)


def _pair_diag(upd):
    return upd[:QK_DIM, :V_DIM], upd[QK_DIM:, V_DIM:]


def _shift_rows(x, k):
    k = k % x.shape[0]
    return x if k == 0 else pltpu.roll(x, k, axis=0)


def _halo_specs(tile, seq_len, tile_of_step):
    hpt = tile // HALO
    n_halo = seq_len // HALO
    prev = pl.BlockSpec((None, HALO, D_MODEL),
                        lambda bi, j: (bi, jnp.maximum(tile_of_step(j) * hpt - 1, 0), 0))
    nxt = pl.BlockSpec((None, HALO, D_MODEL),
                       lambda bi, j: (bi, jnp.minimum((tile_of_step(j) + 1) * hpt, n_halo - 1), 0))
    return prev, nxt


def _ada_kernel(c_ref, w_ref, b_ref, o_ref):
    c = c_ref[...]
    a = (c * _sigmoid(c)).astype(_BF16)
    o_ref[...] = _dot(a, w_ref[...]) + b_ref[...]


def _ada_call(c, w_ada_b, b_ada):
    b = c.shape[0]
    n = w_ada_b.shape[1]
    nb = D_MODEL
    return pl.pallas_call(
        _ada_kernel,
        grid=(n // nb,),
        in_specs=[pl.BlockSpec((b, D_MODEL), lambda j: (0, 0)),
                  pl.BlockSpec((D_MODEL, nb), lambda j: (0, j)),
                  pl.BlockSpec((1, nb), lambda j: (0, j))],
        out_specs=pl.BlockSpec((b, nb), lambda j: (0, j)),
        out_shape=jax.ShapeDtypeStruct((b, n), _F32),
        name="ada",
    )(c, w_ada_b, b_ada.reshape(1, n))


def _rotary(t, cos, sin_signed, first_half):
    partner = jnp.where(first_half, pltpu.roll(t, 96, axis=1), pltpu.roll(t, 32, axis=1))
    return t * cos + partner * sin_signed


def _proj_kernel(x_ref, xprev_ref, xnext_ref, mod_ref, g_ref, w_ref, cos_ref, sin_ref, invc_ref,
                 dkbt_ref, gcb_ref, gain_ref, pscale_ref,
                 q_ref, kt_ref, v_ref, gs_ref, pooled_ref, gbs_ref, sb_ref,
                 state_ref, ktf_ref, *, tile):
    n_chunks = tile // CHUNK
    ext = tile + 2 * HALO
    step = pl.program_id(1)
    t_idx = pl.num_programs(1) - 1 - step

    @pl.when(step == 0)
    def _():
        state_ref[...] = jnp.zeros_like(state_ref)

    shift = mod_ref[0:1, :]
    gain = g_ref[...] * (1.0 + mod_ref[1:2, :])
    h = _adaln_rmsnorm(x_ref[...], gain, shift)
    h_prev = jnp.where(t_idx > 0, _adaln_rmsnorm(xprev_ref[...], gain, shift), 0.0)
    h_next = jnp.where(t_idx < pl.num_programs(1) - 1, _adaln_rmsnorm(xnext_ref[...], gain, shift), 0.0)
    hb = h.astype(_BF16)
    hb_ext = jnp.concatenate([h_prev, h, h_next], axis=0).astype(_BF16)

    def proj(off, width):
        return _dot_two_row_parts(hb, w_ref[:, off:off + width])

    for j in range(POOL_WIDTH // 256):
        pe2 = _dot_two_row_parts(hb_ext, w_ref[:, OFF_P + 256 * j:OFF_P + 256 * (j + 1)])
        for hh in range(2):
            g = 2 * j + hh
            w = POOL_WINDOWS[g]
            pe = pe2[:, 128 * hh:128 * (hh + 1)]
            acc = pe
            span = 1
            while span < w:
                acc = acc + _shift_rows(acc, span)
                span *= 2
            acc = _shift_rows(acc, ext - (w // 2 - 1))[HALO:HALO + tile, :]
            pooled = acc * invc_ref[g] - pe[HALO:HALO + tile, :]
            pooled_ref[:, POOL_GROUP_IN * g:POOL_GROUP_IN * (g + 1)] = pooled.astype(_BF16)

    lane = lax.broadcasted_iota(jnp.int32, (tile, 128), 1)
    first_half = (lane % QK_DIM) < (QK_DIM // 2)
    cos = cos_ref[...]
    sin = sin_ref[...]
    for j in range(RET_QK // 256):
        q2 = proj(OFF_Q + 256 * j, 256)
        k2 = proj(OFF_K + 256 * j, 256)
        for hh in range(2):
            cols = slice(256 * j + 128 * hh, 256 * j + 128 * (hh + 1))
            half = slice(128 * hh, 128 * (hh + 1))
            q_ref[:, cols] = _rotary(q2[:, half], cos, sin, first_half).astype(_BF16)
            kt = (_rotary(k2[:, half], cos, sin, first_half) * (QK_DIM ** -0.5)).T
            ktf_ref[cols, :] = kt
            kt_ref[cols, :] = kt.astype(_BF16)

    for j in range(RET_V // 256):
        cols = slice(256 * j, 256 * (j + 1))
        v_ref[:, cols] = proj(OFF_V + 256 * j, 256).astype(_BF16)
        g = proj(OFF_GSW + 256 * j, 256)
        ga = _sigmoid(proj(OFF_GA + 256 * j, 256))
        gs_ref[:, cols] = (ga * ((g * _sigmoid(g)) * gain_ref[:, cols])).astype(_BF16)
        gb = _sigmoid(proj(OFF_GB + 256 * j, 256))
        gbs_ref[:, cols] = (gb * pscale_ref[:, cols]).astype(_BF16)

    for c in reversed(range(n_chunks)):
        toks = slice(c * CHUNK, (c + 1) * CHUNK)
        for pr in range(N_PAIRS):
            prow = slice(128 * pr, 128 * (pr + 1))
            kdt = (ktf_ref[prow, toks] * dkbt_ref[prow, :]).astype(_BF16)
            upd = _pair_diag(_dot(kdt, v_ref[toks, 256 * pr:256 * (pr + 1)]))
            for hh in range(2):
                hd = 2 * pr + hh
                s_old = state_ref[hd]
                sb_ref[c, hd] = s_old.astype(_BF16)
                state_ref[hd] = s_old * gcb_ref[hd] + upd[hh]


def _proj_call(x, mod, g_mix, w_in_b, cos_t, sin_t, invc, dkbt, gcb, gain, pool_scale, *, tile):
    b, s, _ = x.shape
    nt = s // tile
    nc = s // CHUNK
    cpt = tile // CHUNK

    def tile_of_step(j):
        return nt - 1 - j

    def tok_spec(width):
        return pl.BlockSpec((None, tile, width), lambda bi, j: (bi, tile_of_step(j), 0))

    def kt_spec():
        return pl.BlockSpec((None, RET_QK, tile), lambda bi, j: (bi, 0, tile_of_step(j)))

    def edge_variant(bi, j):
        t = tile_of_step(j)
        return ((t == 0).astype(jnp.int32) + 2 * (t == nt - 1).astype(jnp.int32), 0, 0, 0)

    out_shape = (
        jax.ShapeDtypeStruct((b, s, RET_QK), _BF16),
        jax.ShapeDtypeStruct((b, RET_QK, s), _BF16),
        jax.ShapeDtypeStruct((b, s, RET_V), _BF16),
        jax.ShapeDtypeStruct((b, s, RET_V), _BF16),
        jax.ShapeDtypeStruct((b, s, POOL_WIDTH), _BF16),
        jax.ShapeDtypeStruct((b, s, D_MODEL), _BF16),
        jax.ShapeDtypeStruct((b, nc, N_HEADS, QK_DIM, V_DIM), _BF16),
    )
    out_specs = (
        tok_spec(RET_QK), kt_spec(), tok_spec(RET_V), tok_spec(RET_V),
        tok_spec(POOL_WIDTH), tok_spec(D_MODEL),
        pl.BlockSpec((None, cpt, N_HEADS, QK_DIM, V_DIM),
                     lambda bi, j: (bi, tile_of_step(j), 0, 0, 0)),
    )
    prev_spec, next_spec = _halo_specs(tile, s, tile_of_step)
    in_specs = [
        tok_spec(D_MODEL), prev_spec, next_spec,
        pl.BlockSpec((None, N_MOD, D_MODEL), lambda bi, j: (bi, 0, 0)),
        _const_spec((1, D_MODEL)),
        _const_spec((D_MODEL, IN_WIDTH)),
        pl.BlockSpec((tile, 128), lambda bi, j: (tile_of_step(j), 0)),
        pl.BlockSpec((tile, 128), lambda bi, j: (tile_of_step(j), 0)),
        pl.BlockSpec((None, N_POOL_GROUPS, tile, POOL_GROUP_IN), edge_variant),
        _const_spec((RET_QK, CHUNK)),
        _const_spec((N_HEADS, 1, V_DIM)),
        _const_spec((1, RET_V)),
        _const_spec((1, D_MODEL)),
    ]
    return pl.pallas_call(
        functools.partial(_proj_kernel, tile=tile),
        grid=(b, nt),
        in_specs=in_specs,
        out_specs=out_specs,
        out_shape=out_shape,
        scratch_shapes=[pltpu.VMEM((N_HEADS, QK_DIM, V_DIM), _F32),
                        pltpu.VMEM((RET_QK, tile), _F32)],
        compiler_params=pltpu.CompilerParams(
            dimension_semantics=("arbitrary", "arbitrary"),
            vmem_limit_bytes=VMEM_LIMIT_BYTES),
        name="proj",
    )(x, x, x, mod, g_mix, w_in_b, cos_t, sin_t, invc, dkbt, gcb, gain, pool_scale)


def _mix_kernel(mod_ref, q_ref, kt_ref, v_ref, gs_ref, pooled_ref, gbs_ref, sb_ref,
                dpair_ref, dqf_ref, dqb_ref, dkft_ref, gcf_ref, wpool_ref, wout_ref,
                o_ref, state_ref, mb_ref, merged_ref, *, tile):
    n_chunks = tile // CHUNK

    @pl.when(pl.program_id(1) == 0)
    def _():
        state_ref[...] = jnp.zeros_like(state_ref)

    for g in range(N_POOL_GROUPS):
        ocols = slice(POOL_GROUP_OUT * g, POOL_GROUP_OUT * (g + 1))
        bb = _dot_two_row_parts(pooled_ref[:, POOL_GROUP_IN * g:POOL_GROUP_IN * (g + 1)], wpool_ref[g])
        mb_ref[:, ocols] = gbs_ref[:, ocols].astype(_F32) * bb

    zeros_v = jnp.zeros((CHUNK, V_DIM), _BF16)
    zeros_s = jnp.zeros((QK_DIM, V_DIM), _BF16)
    zeros_k = jnp.zeros((QK_DIM, CHUNK), _BF16)

    def pair_state(sa, sb):
        return jnp.concatenate([jnp.concatenate([sa, zeros_s], axis=1),
                                jnp.concatenate([zeros_s, sb], axis=1)], axis=0)

    for c in range(n_chunks):
        toks = slice(c * CHUNK, (c + 1) * CHUNK)
        for pr in range(N_PAIRS):
            cols = slice(128 * pr, 128 * (pr + 1))
            vcols = slice(256 * pr, 256 * (pr + 1))
            ha, hb = 2 * pr, 2 * pr + 1
            qp = q_ref[toks, cols]
            kt = kt_ref[cols, toks]
            vp = v_ref[toks, vcols]
            qf32 = qp.astype(_F32)
            kt_heads = jnp.concatenate([jnp.concatenate([kt[:QK_DIM], zeros_k], axis=0),
                                        jnp.concatenate([zeros_k, kt[QK_DIM:]], axis=0)], axis=1)
            scores = _dot(qp, kt_heads)
            pmat = (scores * dpair_ref[pr]).astype(_BF16)
            qf = (qf32 * dqf_ref[:, cols]).astype(_BF16)
            qb = (qf32 * dqb_ref[:, cols]).astype(_BF16)
            lhs = jnp.concatenate([pmat, qf, qb], axis=1)
            vbd = jnp.concatenate(
                [jnp.concatenate([vp[:, :V_DIM], zeros_v], axis=1),
                 jnp.concatenate([zeros_v, vp[:, V_DIM:]], axis=1)], axis=0)
            sf_a = state_ref[ha]
            sf_b = state_ref[hb]
            rhs = jnp.concatenate([vbd,
                                   pair_state(sf_a.astype(_BF16), sf_b.astype(_BF16)),
                                   pair_state(sb_ref[c, ha], sb_ref[c, hb])], axis=0)
            o = _dot(lhs, rhs)
            kdt = (kt.astype(_F32) * dkft_ref[cols, :]).astype(_BF16)
            upd_a, upd_b = _pair_diag(_dot(kdt, vp))
            state_ref[ha] = sf_a * gcf_ref[ha] + upd_a
            state_ref[hb] = sf_b * gcf_ref[hb] + upd_b
            normed = []
            for hh in range(2):
                oh = o[:, V_DIM * hh:V_DIM * (hh + 1)]
                d = oh - jnp.mean(oh, axis=-1, keepdims=True)
                var = jnp.mean(d * d, axis=-1, keepdims=True)
                normed.append(d * lax.rsqrt(var + EPS))
            merged = (gs_ref[toks, vcols].astype(_F32) * jnp.concatenate(normed, axis=1)
                      + mb_ref[toks, vcols])
            merged_ref[toks, vcols] = merged.astype(_BF16)

    o_ref[...] = mod_ref[2:3, :] * _dot_two_row_parts(merged_ref[...], wout_ref[...])


def _mix_call(mod, q, kt, v, gs, pooled, gbs, sb, dpair, dqf, dqb, dkft, gcf,
              w_pool_b, w_out_b, *, tile):
    b, s, _ = q.shape
    nt = s // tile
    cpt = tile // CHUNK

    def tok_spec(width):
        return pl.BlockSpec((None, tile, width), lambda bi, j: (bi, j, 0))

    def kt_spec():
        return pl.BlockSpec((None, RET_QK, tile), lambda bi, j: (bi, 0, j))

    in_specs = [
        pl.BlockSpec((None, N_MOD, D_MODEL), lambda bi, j: (bi, 0, 0)),
        tok_spec(RET_QK), kt_spec(), tok_spec(RET_V), tok_spec(RET_V),
        tok_spec(POOL_WIDTH), tok_spec(D_MODEL),
        pl.BlockSpec((None, cpt, N_HEADS, QK_DIM, V_DIM), lambda bi, j: (bi, j, 0, 0, 0)),
        _const_spec((N_PAIRS, CHUNK, 2 * CHUNK)),
        _const_spec((CHUNK, RET_QK)), _const_spec((CHUNK, RET_QK)), _const_spec((RET_QK, CHUNK)),
        _const_spec((N_HEADS, 1, V_DIM)),
        _const_spec((N_POOL_GROUPS, POOL_GROUP_IN, POOL_GROUP_OUT)),
        _const_spec((D_MODEL, D_MODEL)),
    ]
    return pl.pallas_call(
        functools.partial(_mix_kernel, tile=tile),
        grid=(b, nt),
        in_specs=in_specs,
        out_specs=tok_spec(D_MODEL),
        out_shape=jax.ShapeDtypeStruct((b, s, D_MODEL), _F32),
        scratch_shapes=[pltpu.VMEM((N_HEADS, QK_DIM, V_DIM), _F32),
                        pltpu.VMEM((tile, D_MODEL), _F32),
                        pltpu.VMEM((tile, D_MODEL), _BF16)],
        compiler_params=pltpu.CompilerParams(
            dimension_semantics=("arbitrary", "arbitrary"),
            vmem_limit_bytes=VMEM_LIMIT_BYTES),
        name="mix",
    )(mod, q, kt, v, gs, pooled, gbs, sb, dpair, dqf, dqb, dkft, gcf, w_pool_b, w_out_b)


def _gelu_tanh(x):
    c = math.sqrt(2.0 / math.pi)
    half_x = 0.5 * x
    return half_x + half_x * jnp.tanh(x * (c + (c * 0.044715) * (x * x)))


def _ffn_kernel(x_ref, xprev_ref, xnext_ref, m_ref, mprev_ref, mnext_ref, mod_ref, g_ref, wup_ref,
                cw_ref, cb_ref, wdown_ref, gfin_ref, o_ref, x1_ref, act_head_ref, act_tail_ref, *, tile):
    t_idx = pl.program_id(1)
    ext = tile + 2 * HALO
    shift = mod_ref[3:4, :]
    gain = g_ref[...] * (1.0 + mod_ref[4:5, :])
    gate = mod_ref[5:6, :]

    x1 = x_ref[...] + m_ref[...]
    x1_ref[...] = x1
    h_prev = jnp.where(t_idx > 0,
                       _adaln_rmsnorm(xprev_ref[...] + mprev_ref[...], gain, shift), 0.0)
    h_next = jnp.where(t_idx < pl.num_programs(1) - 1,
                       _adaln_rmsnorm(xnext_ref[...] + mnext_ref[...], gain, shift), 0.0)
    hb = jnp.concatenate([_adaln_rmsnorm(x1, gain, shift), h_next, h_prev], axis=0).astype(_BF16)

    def conv(u, cols):
        out = (_shift_rows(u, 1) * cw_ref[0:1, cols] + u * cw_ref[1:2, cols]
               + _shift_rows(u, ext - 1) * cw_ref[2:3, cols])
        return out[:tile, :] + cb_ref[:, cols]

    def up(cols):
        return _dot_two_row_parts(hb, wup_ref[:, cols])

    n_blocks = D_FF // FF_BLOCK
    n_head = n_blocks - FF_TAIL_BLOCKS
    for j in range(n_blocks):
        ca = slice(FF_BLOCK * j, FF_BLOCK * (j + 1))
        cl = slice(D_FF + FF_BLOCK * j, D_FF + FF_BLOCK * (j + 1))
        ua = up(ca)
        ul = up(cl)
        act = (_gelu_tanh(conv(ua, ca)) * conv(ul, cl)).astype(_BF16)
        if j < n_head:
            act_head_ref[:, ca] = act
        else:
            jt = j - n_head
            act_tail_ref[:, FF_BLOCK * jt:FF_BLOCK * (jt + 1)] = act

    k_head = n_head * FF_BLOCK
    ffn = (_dot_two_row_parts(act_head_ref[...], wdown_ref[:k_head, :])
           + _dot_two_row_parts(act_tail_ref[...], wdown_ref[k_head:, :]))
    y = x1_ref[...] + gate * ffn
    o_ref[...] = y * _rms_scale(y) * gfin_ref[...]


def _ffn_call(x, m, mod, g_ffn, w_up_b, conv_w, conv_b, w_down_b, g_final, *, tile):
    b, s, _ = x.shape
    nt = s // tile
    n_head = D_FF // FF_BLOCK - FF_TAIL_BLOCKS

    tok_spec = pl.BlockSpec((None, tile, D_MODEL), lambda bi, j: (bi, j, 0))
    prev_spec, next_spec = _halo_specs(tile, s, lambda j: j)
    in_specs = [
        tok_spec, prev_spec, next_spec, tok_spec, prev_spec, next_spec,
        pl.BlockSpec((None, N_MOD, D_MODEL), lambda bi, j: (bi, 0, 0)),
        _const_spec((1, D_MODEL)),
        _const_spec((D_MODEL, 2 * D_FF)),
        _const_spec((3, 2 * D_FF)),
        _const_spec((1, 2 * D_FF)),
        _const_spec((D_FF, D_MODEL)),
        _const_spec((1, D_MODEL)),
    ]
    return pl.pallas_call(
        functools.partial(_ffn_kernel, tile=tile),
        grid=(b, nt),
        in_specs=in_specs,
        out_specs=tok_spec,
        out_shape=jax.ShapeDtypeStruct((b, s, D_MODEL), _F32),
        scratch_shapes=[pltpu.VMEM((tile, D_MODEL), _F32),
                        pltpu.VMEM((tile, n_head * FF_BLOCK), _BF16),
                        pltpu.VMEM((tile, FF_TAIL_BLOCKS * FF_BLOCK), _BF16)],
        compiler_params=pltpu.CompilerParams(
            dimension_semantics=("arbitrary", "arbitrary"),
            vmem_limit_bytes=VMEM_LIMIT_BYTES),
        name="ffn",
    )(x, x, x, m, m, m, mod, g_ffn, w_up_b, conv_w, conv_b, w_down_b, g_final)


def _rope_tables(s):
    half = QK_DIM // 2
    inv_freq = ROPE_BASE ** (-jnp.arange(half, dtype=_F32) / half)
    ang = jnp.arange(s, dtype=_F32)[:, None] * inv_freq[None, :]
    cos = jnp.tile(jnp.cos(ang), (1, 4))
    sin = jnp.tile(jnp.sin(ang), (1, 4))
    first_half = (jnp.arange(128) % QK_DIM) < half
    return cos, jnp.where(first_half[None, :], -sin, sin)


def _pool_inv_counts(tile):
    row = jnp.arange(tile)
    variants = []
    for v in range(4):
        per_group = []
        for w in POOL_WINDOWS:
            lo = row - w // 2
            hi = row + w // 2
            if v & 1:
                lo = jnp.maximum(lo, 0)
            if v & 2:
                hi = jnp.minimum(hi, tile)
            inv = 1.0 / (hi - lo).astype(_F32)
            per_group.append(jnp.broadcast_to(inv[:, None], (tile, POOL_GROUP_IN)))
        variants.append(jnp.stack(per_group))
    return jnp.stack(variants)


def _decay_tables(ret_decay_f, ret_decay_b):
    lg_f = jax.nn.log_sigmoid(ret_decay_f.astype(_F32))
    lg_b = jax.nn.log_sigmoid(ret_decay_b.astype(_F32))
    idx = jnp.arange(CHUNK, dtype=_F32)
    diff = idx[:, None] - idx[None, :]
    causal = diff >= 0.0
    d_f = jnp.exp(jnp.where(causal, diff, 0.0)[None] * lg_f[:, None, None])
    d_b = jnp.exp(jnp.where(causal, 0.0, -diff)[None] * lg_b[:, None, None])
    d = jnp.where(causal[None], d_f, d_b)
    dpair = d.reshape(N_PAIRS, 2, CHUNK, CHUNK).transpose(0, 2, 1, 3).reshape(N_PAIRS, CHUNK, 2 * CHUNK)

    def per_lane(power, lg):
        return jnp.repeat(jnp.exp(power[:, None] * lg[None, :]), QK_DIM, axis=1)

    def chunk_decay(lg):
        return jnp.broadcast_to(jnp.exp(CHUNK * lg)[:, None, None], (N_HEADS, 1, V_DIM))

    return dict(dpair=dpair, dqf=per_lane(idx + 1.0, lg_f), dqb=per_lane(CHUNK - idx, lg_b),
                dkft=per_lane(CHUNK - 1.0 - idx, lg_f).T, dkbt=per_lane(idx, lg_b).T,
                gcf=chunk_decay(lg_f), gcb=chunk_decay(lg_b))


def _encoder(x, c, w, *, tile):
    b, s, _ = x.shape
    mod = _ada_call(c, w["w_ada"], w["b_ada"]).reshape(b, N_MOD, D_MODEL)
    cos_t, sin_t = _rope_tables(s)
    q, kt, v, gs, pooled, gbs, sb = _proj_call(
        x, mod, w["g_mix"], w["w_in"], cos_t, sin_t, _pool_inv_counts(tile), w["dkbt"], w["gcb"],
        w["gain"], w["pool_scale"], tile=tile)
    m = _mix_call(mod, q, kt, v, gs, pooled, gbs, sb, w["dpair"], w["dqf"], w["dqb"],
                  w["dkft"], w["gcf"], w["w_pool"], w["w_out"], tile=tile)
    return _ffn_call(x, m, mod, w["g_ffn"], w["w_up"], w["conv_w"], w["conv_b"], w["w_down"],
                     w["g_final"], tile=tile)


def _prepare_weights(w_ada, b_ada, g_mix, w_in, ret_decay_f, ret_decay_b, ret_gn_gain, w_pool_grp,
                     pool_scale, w_out, g_ffn, w_up, conv_w, conv_b, w_down, g_final):
    return dict(
        _decay_tables(ret_decay_f[0], ret_decay_b[0]),
        w_ada=w_ada[0].astype(_BF16), b_ada=b_ada[0],
        g_mix=g_mix[0].reshape(1, D_MODEL), w_in=w_in[0].astype(_BF16),
        gain=ret_gn_gain[0].reshape(1, RET_V), w_pool=w_pool_grp[0].astype(_BF16),
        pool_scale=pool_scale[0].reshape(1, D_MODEL), w_out=w_out[0].astype(_BF16),
        g_ffn=g_ffn[0].reshape(1, D_MODEL), w_up=w_up[0].astype(_BF16),
        conv_w=conv_w[0], conv_b=conv_b[0].reshape(1, 2 * D_FF),
        w_down=w_down[0].astype(_BF16), g_final=g_final.reshape(1, D_MODEL))


def kernel(x_prompt, x_sample, c_prompt, c_sample, w_ada, b_ada, g_mix, w_in, ret_decay_f,
           ret_decay_b, ret_gn_gain, w_pool_grp, pool_scale, w_out, g_ffn, w_up, conv_w, conv_b,
           w_down, g_final):
    assert w_ada.shape[0] == 1, "single-layer trunk"
    w = _prepare_weights(w_ada, b_ada, g_mix, w_in, ret_decay_f, ret_decay_b, ret_gn_gain,
                         w_pool_grp, pool_scale, w_out, g_ffn, w_up, conv_w, conv_b, w_down, g_final)
    y_prompt = _encoder(x_prompt, c_prompt, w, tile=min(512, x_prompt.shape[1]))
    y_sample = _encoder(x_sample, c_sample, w, tile=min(512, x_sample.shape[1]))
    return (y_prompt, y_sample)
```

```python
import functools
import math

import jax
import jax.numpy as jnp
from jax import lax
from jax.experimental import pallas as pl
from jax.experimental.pallas import tpu as pltpu

D_MODEL = 1024
N_HEADS = 8
QK_DIM = 64
V_DIM = 128
RET_QK = N_HEADS * QK_DIM
RET_V = N_HEADS * V_DIM
N_PAIRS = N_HEADS // 2
CHUNK = 128
ROPE_BASE = 10000.0
POOL_WINDOWS = (2, 4, 8, 16)
N_POOL_GROUPS = 4
POOL_WIDTH = 512
POOL_GROUP_IN = 128
POOL_GROUP_OUT = 256
D_FF = 2816
N_MOD = 6
EPS = 1e-6
IN_WIDTH = 2 * RET_QK + 2 * RET_V + POOL_WIDTH + 2 * D_MODEL
OFF_Q, OFF_K, OFF_V, OFF_GSW, OFF_P, OFF_GA, OFF_GB = 0, 512, 1024, 2048, 3072, 3584, 4608

HALO = 8
FF_BLOCK = 256
FF_TAIL_BLOCKS = 2
BF16_ROWS = 16
DOT_ROW_PARTS = 4
VMEM_LIMIT_BYTES = 56 * 1024 * 1024

_F32 = jnp.float32
_BF16 = jnp.bfloat16


def _const_spec(shape):
    nd = len(shape)
    return pl.BlockSpec(shape, lambda *_: (0,) * nd, pipeline_mode=pl.Buffered(1))


def _sigmoid(x):
    return 1.0 / (1.0 + jnp.exp(-x))


def _rms_scale(x):
    return lax.rsqrt(jnp.mean(x * x, axis=-1, keepdims=True) + EPS)


def _adaln_rmsnorm(v, gain, shift):
    return (v * _rms_scale(v)) * gain + shift


def _dot(a, b):
    return jnp.dot(a, b, preferred_element_type=_F32)


def _dot_row_parts(a, b):
    rows = a.shape[0]
    step = (rows // (DOT_ROW_PARTS * BF16_ROWS)) * BF16_ROWS
    bounds = [i * step for i in range(DOT_ROW_PARTS)] + [rows]
    return jnp.concatenate([_dot(a[lo:hi], b) for lo, hi in zip(bounds, bounds[1:])], axis=0)


def _pair_diag(upd):
    return upd[:QK_DIM, :V_DIM], upd[QK_DIM:, V_DIM:]


def _shift_rows(x, k):
    k = k % x.shape[0]
    return x if k == 0 else pltpu.roll(x, k, axis=0)


def _halo_specs(tile, seq_len, tile_of_step):
    hpt = tile // HALO
    n_halo = seq_len // HALO
    prev = pl.BlockSpec((None, HALO, D_MODEL),
                        lambda bi, j: (bi, jnp.maximum(tile_of_step(j) * hpt - 1, 0), 0))
    nxt = pl.BlockSpec((None, HALO, D_MODEL),
                       lambda bi, j: (bi, jnp.minimum((tile_of_step(j) + 1) * hpt, n_halo - 1), 0))
    return prev, nxt


def _ada_kernel(c_ref, w_ref, b_ref, o_ref):
    c = c_ref[...]
    a = (c * _sigmoid(c)).astype(_BF16)
    o_ref[...] = _dot(a, w_ref[...]) + b_ref[...]


def _ada_call(c, w_ada_b, b_ada):
    b = c.shape[0]
    n = w_ada_b.shape[1]
    nb = D_MODEL
    return pl.pallas_call(
        _ada_kernel,
        grid=(n // nb,),
        in_specs=[pl.BlockSpec((b, D_MODEL), lambda j: (0, 0)),
                  pl.BlockSpec((D_MODEL, nb), lambda j: (0, j)),
                  pl.BlockSpec((1, nb), lambda j: (0, j))],
        out_specs=pl.BlockSpec((b, nb), lambda j: (0, j)),
        out_shape=jax.ShapeDtypeStruct((b, n), _F32),
        name="ada",
    )(c, w_ada_b, b_ada.reshape(1, n))


def _rotary(t, cos, sin_signed, first_half):
    partner = jnp.where(first_half, pltpu.roll(t, 96, axis=1), pltpu.roll(t, 32, axis=1))
    return t * cos + partner * sin_signed


def _proj_kernel(x_ref, xprev_ref, xnext_ref, mod_ref, g_ref, w_ref, cos_ref, sin_ref, invc_ref,
                 dkbt_ref, gcb_ref, gain_ref, pscale_ref,
                 q_ref, kt_ref, v_ref, gs_ref, pooled_ref, gbs_ref, sb_ref,
                 state_ref, ktf_ref, *, tile):
    n_chunks = tile // CHUNK
    ext = tile + 2 * HALO
    step = pl.program_id(1)
    t_idx = pl.num_programs(1) - 1 - step

    @pl.when(step == 0)
    def _():
        state_ref[...] = jnp.zeros_like(state_ref)

    shift = mod_ref[0:1, :]
    gain = g_ref[...] * (1.0 + mod_ref[1:2, :])
    h = _adaln_rmsnorm(x_ref[...], gain, shift)
    h_prev = jnp.where(t_idx > 0, _adaln_rmsnorm(xprev_ref[...], gain, shift), 0.0)
    h_next = jnp.where(t_idx < pl.num_programs(1) - 1, _adaln_rmsnorm(xnext_ref[...], gain, shift), 0.0)
    hb = h.astype(_BF16)
    hb_ext = jnp.concatenate([h_prev, h, h_next], axis=0).astype(_BF16)

    def proj(off, width):
        return _dot_row_parts(hb, w_ref[:, off:off + width])

    for j in range(POOL_WIDTH // 256):
        pe2 = _dot_row_parts(hb_ext, w_ref[:, OFF_P + 256 * j:OFF_P + 256 * (j + 1)])
        for hh in range(2):
            g = 2 * j + hh
            w = POOL_WINDOWS[g]
            pe = pe2[:, 128 * hh:128 * (hh + 1)]
            acc = pe
            span = 1
            while span < w:
                acc = acc + _shift_rows(acc, span)
                span *= 2
            acc = _shift_rows(acc, ext - (w // 2 - 1))[HALO:HALO + tile, :]
            pooled = acc * invc_ref[g] - pe[HALO:HALO + tile, :]
            pooled_ref[:, POOL_GROUP_IN * g:POOL_GROUP_IN * (g + 1)] = pooled.astype(_BF16)

    lane = lax.broadcasted_iota(jnp.int32, (tile, 128), 1)
    first_half = (lane % QK_DIM) < (QK_DIM // 2)
    cos = cos_ref[...]
    sin = sin_ref[...]
    for j in range(RET_QK // 256):
        q2 = proj(OFF_Q + 256 * j, 256)
        k2 = proj(OFF_K + 256 * j, 256)
        for hh in range(2):
            cols = slice(256 * j + 128 * hh, 256 * j + 128 * (hh + 1))
            half = slice(128 * hh, 128 * (hh + 1))
            q_ref[:, cols] = _rotary(q2[:, half], cos, sin, first_half).astype(_BF16)
            kt = (_rotary(k2[:, half], cos, sin, first_half) * (QK_DIM ** -0.5)).T
            ktf_ref[cols, :] = kt
            kt_ref[cols, :] = kt.astype(_BF16)

    for j in range(RET_V // 256):
        cols = slice(256 * j, 256 * (j + 1))
        v_ref[:, cols] = proj(OFF_V + 256 * j, 256).astype(_BF16)
        g = proj(OFF_GSW + 256 * j, 256)
        ga = _sigmoid(proj(OFF_GA + 256 * j, 256))
        gs_ref[:, cols] = (ga * ((g * _sigmoid(g)) * gain_ref[:, cols])).astype(_BF16)
        gb = _sigmoid(proj(OFF_GB + 256 * j, 256))
        gbs_ref[:, cols] = (gb * pscale_ref[:, cols]).astype(_BF16)

    for c in reversed(range(n_chunks)):
        toks = slice(c * CHUNK, (c + 1) * CHUNK)
        for pr in range(N_PAIRS):
            prow = slice(128 * pr, 128 * (pr + 1))
            kdt = (ktf_ref[prow, toks] * dkbt_ref[prow, :]).astype(_BF16)
            upd = _pair_diag(_dot(kdt, v_ref[toks, 256 * pr:256 * (pr + 1)]))
            for hh in range(2):
                hd = 2 * pr + hh
                s_old = state_ref[hd]
                sb_ref[c, hd] = s_old.astype(_BF16)
                state_ref[hd] = s_old * gcb_ref[hd] + upd[hh]


def _proj_call(x, mod, g_mix, w_in_b, cos_t, sin_t, invc, dkbt, gcb, gain, pool_scale, *, tile):
    b, s, _ = x.shape
    nt = s // tile
    nc = s // CHUNK
    cpt = tile // CHUNK

    def tile_of_step(j):
        return nt - 1 - j

    def tok_spec(width):
        return pl.BlockSpec((None, tile, width), lambda bi, j: (bi, tile_of_step(j), 0))

    def kt_spec():
        return pl.BlockSpec((None, RET_QK, tile), lambda bi, j: (bi, 0, tile_of_step(j)))

    def edge_variant(bi, j):
        t = tile_of_step(j)
        return ((t == 0).astype(jnp.int32) + 2 * (t == nt - 1).astype(jnp.int32), 0, 0, 0)

    out_shape = (
        jax.ShapeDtypeStruct((b, s, RET_QK), _BF16),
        jax.ShapeDtypeStruct((b, RET_QK, s), _BF16),
        jax.ShapeDtypeStruct((b, s, RET_V), _BF16),
        jax.ShapeDtypeStruct((b, s, RET_V), _BF16),
        jax.ShapeDtypeStruct((b, s, POOL_WIDTH), _BF16),
        jax.ShapeDtypeStruct((b, s, D_MODEL), _BF16),
        jax.ShapeDtypeStruct((b, nc, N_HEADS, QK_DIM, V_DIM), _BF16),
    )
    out_specs = (
        tok_spec(RET_QK), kt_spec(), tok_spec(RET_V), tok_spec(RET_V),
        tok_spec(POOL_WIDTH), tok_spec(D_MODEL),
        pl.BlockSpec((None, cpt, N_HEADS, QK_DIM, V_DIM),
                     lambda bi, j: (bi, tile_of_step(j), 0, 0, 0)),
    )
    prev_spec, next_spec = _halo_specs(tile, s, tile_of_step)
    in_specs = [
        tok_spec(D_MODEL), prev_spec, next_spec,
        pl.BlockSpec((None, N_MOD, D_MODEL), lambda bi, j: (bi, 0, 0)),
        _const_spec((1, D_MODEL)),
        _const_spec((D_MODEL, IN_WIDTH)),
        pl.BlockSpec((tile, 128), lambda bi, j: (tile_of_step(j), 0)),
        pl.BlockSpec((tile, 128), lambda bi, j: (tile_of_step(j), 0)),
        pl.BlockSpec((None, N_POOL_GROUPS, tile, POOL_GROUP_IN), edge_variant),
        _const_spec((RET_QK, CHUNK)),
        _const_spec((N_HEADS, 1, V_DIM)),
        _const_spec((1, RET_V)),
        _const_spec((1, D_MODEL)),
    ]
    return pl.pallas_call(
        functools.partial(_proj_kernel, tile=tile),
        grid=(b, nt),
        in_specs=in_specs,
        out_specs=out_specs,
        out_shape=out_shape,
        scratch_shapes=[pltpu.VMEM((N_HEADS, QK_DIM, V_DIM), _F32),
                        pltpu.VMEM((RET_QK, tile), _F32)],
        compiler_params=pltpu.CompilerParams(
            dimension_semantics=("arbitrary", "arbitrary"),
            vmem_limit_bytes=VMEM_LIMIT_BYTES),
        name="proj",
    )(x, x, x, mod, g_mix, w_in_b, cos_t, sin_t, invc, dkbt, gcb, gain, pool_scale)


def _mix_kernel(mod_ref, q_ref, kt_ref, v_ref, gs_ref, pooled_ref, gbs_ref, sb_ref,
                dpair_ref, dqf_ref, dqb_ref, dkft_ref, gcf_ref, wpool_ref, wout_ref,
                o_ref, state_ref, mb_ref, merged_ref, *, tile):
    n_chunks = tile // CHUNK

    @pl.when(pl.program_id(1) == 0)
    def _():
        state_ref[...] = jnp.zeros_like(state_ref)

    for g in range(N_POOL_GROUPS):
        ocols = slice(POOL_GROUP_OUT * g, POOL_GROUP_OUT * (g + 1))
        bb = _dot(pooled_ref[:, POOL_GROUP_IN * g:POOL_GROUP_IN * (g + 1)], wpool_ref[g])
        mb_ref[:, ocols] = gbs_ref[:, ocols].astype(_F32) * bb

    zeros_v = jnp.zeros((CHUNK, V_DIM), _BF16)
    zeros_s = jnp.zeros((QK_DIM, V_DIM), _BF16)
    zeros_k = jnp.zeros((QK_DIM, CHUNK), _BF16)

    def pair_state(sa, sb):
        return jnp.concatenate([jnp.concatenate([sa, zeros_s], axis=1),
                                jnp.concatenate([zeros_s, sb], axis=1)], axis=0)

    for c in range(n_chunks):
        toks = slice(c * CHUNK, (c + 1) * CHUNK)
        for pr in range(N_PAIRS):
            cols = slice(128 * pr, 128 * (pr + 1))
            vcols = slice(256 * pr, 256 * (pr + 1))
            ha, hb = 2 * pr, 2 * pr + 1
            qp = q_ref[toks, cols]
            kt = kt_ref[cols, toks]
            vp = v_ref[toks, vcols]
            qf32 = qp.astype(_F32)
            kt_heads = jnp.concatenate([jnp.concatenate([kt[:QK_DIM], zeros_k], axis=0),
                                        jnp.concatenate([zeros_k, kt[QK_DIM:]], axis=0)], axis=1)
            scores = _dot(qp, kt_heads)
            pmat = (scores * dpair_ref[pr]).astype(_BF16)
            qf = (qf32 * dqf_ref[:, cols]).astype(_BF16)
            qb = (qf32 * dqb_ref[:, cols]).astype(_BF16)
            lhs = jnp.concatenate([pmat, qf, qb], axis=1)
            vbd = jnp.concatenate(
                [jnp.concatenate([vp[:, :V_DIM], zeros_v], axis=1),
                 jnp.concatenate([zeros_v, vp[:, V_DIM:]], axis=1)], axis=0)
            sf_a = state_ref[ha]
            sf_b = state_ref[hb]
            rhs = jnp.concatenate([vbd,
                                   pair_state(sf_a.astype(_BF16), sf_b.astype(_BF16)),
                                   pair_state(sb_ref[c, ha], sb_ref[c, hb])], axis=0)
            o = _dot(lhs, rhs)
            kdt = (kt.astype(_F32) * dkft_ref[cols, :]).astype(_BF16)
            upd_a, upd_b = _pair_diag(_dot(kdt, vp))
            state_ref[ha] = sf_a * gcf_ref[ha] + upd_a
            state_ref[hb] = sf_b * gcf_ref[hb] + upd_b
            normed = []
            for hh in range(2):
                oh = o[:, V_DIM * hh:V_DIM * (hh + 1)]
                d = oh - jnp.mean(oh, axis=-1, keepdims=True)
                var = jnp.mean(d * d, axis=-1, keepdims=True)
                normed.append(d * lax.rsqrt(var + EPS))
            merged = (gs_ref[toks, vcols].astype(_F32) * jnp.concatenate(normed, axis=1)
                      + mb_ref[toks, vcols])
            merged_ref[toks, vcols] = merged.astype(_BF16)

    o_ref[...] = mod_ref[2:3, :] * _dot(merged_ref[...], wout_ref[...])


def _mix_call(mod, q, kt, v, gs, pooled, gbs, sb, dpair, dqf, dqb, dkft, gcf,
              w_pool_b, w_out_b, *, tile):
    b, s, _ = q.shape
    nt = s // tile
    cpt = tile // CHUNK

    def tok_spec(width):
        return pl.BlockSpec((None, tile, width), lambda bi, j: (bi, j, 0))

    def kt_spec():
        return pl.BlockSpec((None, RET_QK, tile), lambda bi, j: (bi, 0, j))

    in_specs = [
        pl.BlockSpec((None, N_MOD, D_MODEL), lambda bi, j: (bi, 0, 0)),
        tok_spec(RET_QK), kt_spec(), tok_spec(RET_V), tok_spec(RET_V),
        tok_spec(POOL_WIDTH), tok_spec(D_MODEL),
        pl.BlockSpec((None, cpt, N_HEADS, QK_DIM, V_DIM), lambda bi, j: (bi, j, 0, 0, 0)),
        _const_spec((N_PAIRS, CHUNK, 2 * CHUNK)),
        _const_spec((CHUNK, RET_QK)), _const_spec((CHUNK, RET_QK)), _const_spec((RET_QK, CHUNK)),
        _const_spec((N_HEADS, 1, V_DIM)),
        _const_spec((N_POOL_GROUPS, POOL_GROUP_IN, POOL_GROUP_OUT)),
        _const_spec((D_MODEL, D_MODEL)),
    ]
    return pl.pallas_call(
        functools.partial(_mix_kernel, tile=tile),
        grid=(b, nt),
        in_specs=in_specs,
        out_specs=tok_spec(D_MODEL),
        out_shape=jax.ShapeDtypeStruct((b, s, D_MODEL), _F32),
        scratch_shapes=[pltpu.VMEM((N_HEADS, QK_DIM, V_DIM), _F32),
                        pltpu.VMEM((tile, D_MODEL), _F32),
                        pltpu.VMEM((tile, D_MODEL), _BF16)],
        compiler_params=pltpu.CompilerParams(
            dimension_semantics=("arbitrary", "arbitrary"),
            vmem_limit_bytes=VMEM_LIMIT_BYTES),
        name="mix",
    )(mod, q, kt, v, gs, pooled, gbs, sb, dpair, dqf, dqb, dkft, gcf, w_pool_b, w_out_b)


def _gelu_tanh(x):
    c = math.sqrt(2.0 / math.pi)
    half_x = 0.5 * x
    return half_x + half_x * jnp.tanh(x * (c + (c * 0.044715) * (x * x)))


def _ffn_kernel(x_ref, xprev_ref, xnext_ref, m_ref, mprev_ref, mnext_ref, mod_ref, g_ref, wup_ref,
                cw_ref, cb_ref, wdown_ref, gfin_ref, o_ref, x1_ref, act_head_ref, act_tail_ref, *, tile):
    t_idx = pl.program_id(1)
    ext = tile + 2 * HALO
    shift = mod_ref[3:4, :]
    gain = g_ref[...] * (1.0 + mod_ref[4:5, :])
    gate = mod_ref[5:6, :]

    x1 = x_ref[...] + m_ref[...]
    x1_ref[...] = x1
    h_prev = jnp.where(t_idx > 0,
                       _adaln_rmsnorm(xprev_ref[...] + mprev_ref[...], gain, shift), 0.0)
    h_next = jnp.where(t_idx < pl.num_programs(1) - 1,
                       _adaln_rmsnorm(xnext_ref[...] + mnext_ref[...], gain, shift), 0.0)
    hb = jnp.concatenate([_adaln_rmsnorm(x1, gain, shift), h_next, h_prev], axis=0).astype(_BF16)

    def conv(u, cols):
        out = (_shift_rows(u, 1) * cw_ref[0:1, cols] + u * cw_ref[1:2, cols]
               + _shift_rows(u, ext - 1) * cw_ref[2:3, cols])
        return out[:tile, :] + cb_ref[:, cols]

    def up(cols):
        return _dot_row_parts(hb, wup_ref[:, cols])

    n_blocks = D_FF // FF_BLOCK
    n_head = n_blocks - FF_TAIL_BLOCKS
    for j in range(n_blocks):
        ca = slice(FF_BLOCK * j, FF_BLOCK * (j + 1))
        cl = slice(D_FF + FF_BLOCK * j, D_FF + FF_BLOCK * (j + 1))
        ua = up(ca)
        ul = up(cl)
        act = (_gelu_tanh(conv(ua, ca)) * conv(ul, cl)).astype(_BF16)
        if j < n_head:
            act_head_ref[:, ca] = act
        else:
            jt = j - n_head
            act_tail_ref[:, FF_BLOCK * jt:FF_BLOCK * (jt + 1)] = act

    k_head = n_head * FF_BLOCK
    ffn = (_dot(act_head_ref[...], wdown_ref[:k_head, :])
           + _dot(act_tail_ref[...], wdown_ref[k_head:, :]))
    y = x1_ref[...] + gate * ffn
    o_ref[...] = y * _rms_scale(y) * gfin_ref[...]


def _ffn_call(x, m, mod, g_ffn, w_up_b, conv_w, conv_b, w_down_b, g_final, *, tile):
    b, s, _ = x.shape
    nt = s // tile
    n_head = D_FF // FF_BLOCK - FF_TAIL_BLOCKS

    tok_spec = pl.BlockSpec((None, tile, D_MODEL), lambda bi, j: (bi, j, 0))
    prev_spec, next_spec = _halo_specs(tile, s, lambda j: j)
    in_specs = [
        tok_spec, prev_spec, next_spec, tok_spec, prev_spec, next_spec,
        pl.BlockSpec((None, N_MOD, D_MODEL), lambda bi, j: (bi, 0, 0)),
        _const_spec((1, D_MODEL)),
        _const_spec((D_MODEL, 2 * D_FF)),
        _const_spec((3, 2 * D_FF)),
        _const_spec((1, 2 * D_FF)),
        _const_spec((D_FF, D_MODEL)),
        _const_spec((1, D_MODEL)),
    ]
    return pl.pallas_call(
        functools.partial(_ffn_kernel, tile=tile),
        grid=(b, nt),
        in_specs=in_specs,
        out_specs=tok_spec,
        out_shape=jax.ShapeDtypeStruct((b, s, D_MODEL), _F32),
        scratch_shapes=[pltpu.VMEM((tile, D_MODEL), _F32),
                        pltpu.VMEM((tile, n_head * FF_BLOCK), _BF16),
                        pltpu.VMEM((tile, FF_TAIL_BLOCKS * FF_BLOCK), _BF16)],
        compiler_params=pltpu.CompilerParams(
            dimension_semantics=("arbitrary", "arbitrary"),
            vmem_limit_bytes=VMEM_LIMIT_BYTES),
        name="ffn",
    )(x, x, x, m, m, m, mod, g_ffn, w_up_b, conv_w, conv_b, w_down_b, g_final)


def _rope_tables(s):
    half = QK_DIM // 2
    inv_freq = ROPE_BASE ** (-jnp.arange(half, dtype=_F32) / half)
    ang = jnp.arange(s, dtype=_F32)[:, None] * inv_freq[None, :]
    cos = jnp.tile(jnp.cos(ang), (1, 4))
    sin = jnp.tile(jnp.sin(ang), (1, 4))
    first_half = (jnp.arange(128) % QK_DIM) < half
    return cos, jnp.where(first_half[None, :], -sin, sin)


def _pool_inv_counts(tile):
    row = jnp.arange(tile)
    variants = []
    for v in range(4):
        per_group = []
        for w in POOL_WINDOWS:
            lo = row - w // 2
            hi = row + w // 2
            if v & 1:
                lo = jnp.maximum(lo, 0)
            if v & 2:
                hi = jnp.minimum(hi, tile)
            inv = 1.0 / (hi - lo).astype(_F32)
            per_group.append(jnp.broadcast_to(inv[:, None], (tile, POOL_GROUP_IN)))
        variants.append(jnp.stack(per_group))
    return jnp.stack(variants)


def _decay_tables(ret_decay_f, ret_decay_b):
    lg_f = jax.nn.log_sigmoid(ret_decay_f.astype(_F32))
    lg_b = jax.nn.log_sigmoid(ret_decay_b.astype(_F32))
    idx = jnp.arange(CHUNK, dtype=_F32)
    diff = idx[:, None] - idx[None, :]
    causal = diff >= 0.0
    d_f = jnp.exp(jnp.where(causal, diff, 0.0)[None] * lg_f[:, None, None])
    d_b = jnp.exp(jnp.where(causal, 0.0, -diff)[None] * lg_b[:, None, None])
    d = jnp.where(causal[None], d_f, d_b)
    dpair = d.reshape(N_PAIRS, 2, CHUNK, CHUNK).transpose(0, 2, 1, 3).reshape(N_PAIRS, CHUNK, 2 * CHUNK)

    def per_lane(power, lg):
        return jnp.repeat(jnp.exp(power[:, None] * lg[None, :]), QK_DIM, axis=1)

    def chunk_decay(lg):
        return jnp.broadcast_to(jnp.exp(CHUNK * lg)[:, None, None], (N_HEADS, 1, V_DIM))

    return dict(dpair=dpair, dqf=per_lane(idx + 1.0, lg_f), dqb=per_lane(CHUNK - idx, lg_b),
                dkft=per_lane(CHUNK - 1.0 - idx, lg_f).T, dkbt=per_lane(idx, lg_b).T,
                gcf=chunk_decay(lg_f), gcb=chunk_decay(lg_b))


def _encoder(x, c, w, *, tile):
    b, s, _ = x.shape
    mod = _ada_call(c, w["w_ada"], w["b_ada"]).reshape(b, N_MOD, D_MODEL)
    cos_t, sin_t = _rope_tables(s)
    q, kt, v, gs, pooled, gbs, sb = _proj_call(
        x, mod, w["g_mix"], w["w_in"], cos_t, sin_t, _pool_inv_counts(tile), w["dkbt"], w["gcb"],
        w["gain"], w["pool_scale"], tile=tile)
    m = _mix_call(mod, q, kt, v, gs, pooled, gbs, sb, w["dpair"], w["dqf"], w["dqb"],
                  w["dkft"], w["gcf"], w["w_pool"], w["w_out"], tile=tile)
    return _ffn_call(x, m, mod, w["g_ffn"], w["w_up"], w["conv_w"], w["conv_b"], w["w_down"],
                     w["g_final"], tile=tile)


def _prepare_weights(w_ada, b_ada, g_mix, w_in, ret_decay_f, ret_decay_b, ret_gn_gain, w_pool_grp,
                     pool_scale, w_out, g_ffn, w_up, conv_w, conv_b, w_down, g_final):
    return dict(
        _decay_tables(ret_decay_f[0], ret_decay_b[0]),
        w_ada=w_ada[0].astype(_BF16), b_ada=b_ada[0],
        g_mix=g_mix[0].reshape(1, D_MODEL), w_in=w_in[0].astype(_BF16),
        gain=ret_gn_gain[0].reshape(1, RET_V), w_pool=w_pool_grp[0].astype(_BF16),
        pool_scale=pool_scale[0].reshape(1, D_MODEL), w_out=w_out[0].astype(_BF16),
        g_ffn=g_ffn[0].reshape(1, D_MODEL), w_up=w_up[0].astype(_BF16),
        conv_w=conv_w[0], conv_b=conv_b[0].reshape(1, 2 * D_FF),
        w_down=w_down[0].astype(_BF16), g_final=g_final.reshape(1, D_MODEL))


def kernel(x_prompt, x_sample, c_prompt, c_sample, w_ada, b_ada, g_mix, w_in, ret_decay_f,
           ret_decay_b, ret_gn_gain, w_pool_grp, pool_scale, w_out, g_ffn, w_up, conv_w, conv_b,
           w_down, g_final):
    assert w_ada.shape[0] == 1, "single-layer trunk"
    w = _prepare_weights(w_ada, b_ada, g_mix, w_in, ret_decay_f, ret_decay_b, ret_gn_gain,
                         w_pool_grp, pool_scale, w_out, g_ffn, w_up, conv_w, conv_b, w_down, g_final)
    y_prompt = _encoder(x_prompt, c_prompt, w, tile=min(512, x_prompt.shape[1]))
    y_sample = _encoder(x_sample, c_sample, w, tile=min(512, x_sample.shape[1]))
    return (y_prompt, y_sample)
```

```python
import functools
import math

import jax
import jax.numpy as jnp
from jax import lax
from jax.experimental import pallas as pl
from jax.experimental.pallas import tpu as pltpu

D_MODEL = 1024
N_HEADS = 8
QK_DIM = 64
V_DIM = 128
RET_QK = N_HEADS * QK_DIM
RET_V = N_HEADS * V_DIM
N_PAIRS = N_HEADS // 2
CHUNK = 128
ROPE_BASE = 10000.0
POOL_WINDOWS = (2, 4, 8, 16)
N_POOL_GROUPS = 4
POOL_WIDTH = 512
POOL_GROUP_IN = 128
POOL_GROUP_OUT = 256
D_FF = 2816
N_MOD = 6
EPS = 1e-6
IN_WIDTH = 2 * RET_QK + 2 * RET_V + POOL_WIDTH + 2 * D_MODEL
OFF_Q, OFF_K, OFF_V, OFF_GSW, OFF_P, OFF_GA, OFF_GB = 0, 512, 1024, 2048, 3072, 3584, 4608

HALO = 8
FF_BLOCK = 256
FF_TAIL_BLOCKS = 2
BF16_ROWS = 16
PROJ_ROW_PARTS = 2
FFN_ROW_PARTS = 4
VMEM_LIMIT_BYTES = 56 * 1024 * 1024

_F32 = jnp.float32
_BF16 = jnp.bfloat16


def _const_spec(shape):
    nd = len(shape)
    return pl.BlockSpec(shape, lambda *_: (0,) * nd, pipeline_mode=pl.Buffered(1))


def _sigmoid(x):
    return 1.0 / (1.0 + jnp.exp(-x))


def _rms_scale(x):
    return lax.rsqrt(jnp.mean(x * x, axis=-1, keepdims=True) + EPS)


def _adaln_rmsnorm(v, gain, shift):
    return (v * _rms_scale(v)) * gain + shift


def _dot(a, b):
    return jnp.dot(a, b, preferred_element_type=_F32)


def _dot_row_parts(a, b, n_parts):
    rows = a.shape[0]
    step = (rows // (n_parts * BF16_ROWS)) * BF16_ROWS
    bounds = [i * step for i in range(n_parts)] + [rows]
    return jnp.concatenate([_dot(a[lo:hi], b) for lo, hi in zip(bounds, bounds[1:])], axis=0)


def _pair_diag(upd):
    return upd[:QK_DIM, :V_DIM], upd[QK_DIM:, V_DIM:]


def _shift_rows(x, k):
    k = k % x.shape[0]
    return x if k == 0 else pltpu.roll(x, k, axis=0)


def _halo_specs(tile, seq_len, tile_of_step):
    hpt = tile // HALO
    n_halo = seq_len // HALO
    prev = pl.BlockSpec((None, HALO, D_MODEL),
                        lambda bi, j: (bi, jnp.maximum(tile_of_step(j) * hpt - 1, 0), 0))
    nxt = pl.BlockSpec((None, HALO, D_MODEL),
                       lambda bi, j: (bi, jnp.minimum((tile_of_step(j) + 1) * hpt, n_halo - 1), 0))
    return prev, nxt


def _ada_kernel(c_ref, w_ref, b_ref, o_ref):
    c = c_ref[...]
    a = (c * _sigmoid(c)).astype(_BF16)
    o_ref[...] = _dot(a, w_ref[...]) + b_ref[...]


def _ada_call(c, w_ada_b, b_ada):
    b = c.shape[0]
    n = w_ada_b.shape[1]
    nb = D_MODEL
    return pl.pallas_call(
        _ada_kernel,
        grid=(n // nb,),
        in_specs=[pl.BlockSpec((b, D_MODEL), lambda j: (0, 0)),
                  pl.BlockSpec((D_MODEL, nb), lambda j: (0, j)),
                  pl.BlockSpec((1, nb), lambda j: (0, j))],
        out_specs=pl.BlockSpec((b, nb), lambda j: (0, j)),
        out_shape=jax.ShapeDtypeStruct((b, n), _F32),
        name="ada",
    )(c, w_ada_b, b_ada.reshape(1, n))


def _rotary(t, cos, sin_signed, first_half):
    partner = jnp.where(first_half, pltpu.roll(t, 96, axis=1), pltpu.roll(t, 32, axis=1))
    return t * cos + partner * sin_signed


def _proj_kernel(x_ref, xprev_ref, xnext_ref, mod_ref, g_ref, w_ref, cos_ref, sin_ref, invc_ref,
                 dkbt_ref, gcb_ref, gain_ref, pscale_ref,
                 q_ref, kt_ref, v_ref, gs_ref, pooled_ref, gbs_ref, sb_ref,
                 state_ref, ktf_ref, *, tile):
    n_chunks = tile // CHUNK
    ext = tile + 2 * HALO
    step = pl.program_id(1)
    t_idx = pl.num_programs(1) - 1 - step

    @pl.when(step == 0)
    def _():
        state_ref[...] = jnp.zeros_like(state_ref)

    shift = mod_ref[0:1, :]
    gain = g_ref[...] * (1.0 + mod_ref[1:2, :])
    h = _adaln_rmsnorm(x_ref[...], gain, shift)
    h_prev = jnp.where(t_idx > 0, _adaln_rmsnorm(xprev_ref[...], gain, shift), 0.0)
    h_next = jnp.where(t_idx < pl.num_programs(1) - 1, _adaln_rmsnorm(xnext_ref[...], gain, shift), 0.0)
    hb = h.astype(_BF16)
    hb_ext = jnp.concatenate([h_prev, h, h_next], axis=0).astype(_BF16)

    def proj(off, width):
        return _dot_row_parts(hb, w_ref[:, off:off + width], PROJ_ROW_PARTS)

    for j in range(POOL_WIDTH // 256):
        pe2 = _dot_row_parts(hb_ext, w_ref[:, OFF_P + 256 * j:OFF_P + 256 * (j + 1)], PROJ_ROW_PARTS)
        for hh in range(2):
            g = 2 * j + hh
            w = POOL_WINDOWS[g]
            pe = pe2[:, 128 * hh:128 * (hh + 1)]
            acc = pe
            span = 1
            while span < w:
                acc = acc + _shift_rows(acc, span)
                span *= 2
            acc = _shift_rows(acc, ext - (w // 2 - 1))[HALO:HALO + tile, :]
            pooled = acc * invc_ref[g] - pe[HALO:HALO + tile, :]
            pooled_ref[:, POOL_GROUP_IN * g:POOL_GROUP_IN * (g + 1)] = pooled.astype(_BF16)

    lane = lax.broadcasted_iota(jnp.int32, (tile, 128), 1)
    first_half = (lane % QK_DIM) < (QK_DIM // 2)
    cos = cos_ref[...]
    sin = sin_ref[...]
    for j in range(RET_QK // 256):
        q2 = proj(OFF_Q + 256 * j, 256)
        k2 = proj(OFF_K + 256 * j, 256)
        for hh in range(2):
            cols = slice(256 * j + 128 * hh, 256 * j + 128 * (hh + 1))
            half = slice(128 * hh, 128 * (hh + 1))
            q_ref[:, cols] = _rotary(q2[:, half], cos, sin, first_half).astype(_BF16)
            kt = (_rotary(k2[:, half], cos, sin, first_half) * (QK_DIM ** -0.5)).T
            ktf_ref[cols, :] = kt
            kt_ref[cols, :] = kt.astype(_BF16)

    for j in range(RET_V // 256):
        cols = slice(256 * j, 256 * (j + 1))
        v_ref[:, cols] = proj(OFF_V + 256 * j, 256).astype(_BF16)
        g = proj(OFF_GSW + 256 * j, 256)
        ga = _sigmoid(proj(OFF_GA + 256 * j, 256))
        gs_ref[:, cols] = (ga * ((g * _sigmoid(g)) * gain_ref[:, cols])).astype(_BF16)
        gb = _sigmoid(proj(OFF_GB + 256 * j, 256))
        gbs_ref[:, cols] = (gb * pscale_ref[:, cols]).astype(_BF16)

    for c in reversed(range(n_chunks)):
        toks = slice(c * CHUNK, (c + 1) * CHUNK)
        for pr in range(N_PAIRS):
            prow = slice(128 * pr, 128 * (pr + 1))
            kdt = (ktf_ref[prow, toks] * dkbt_ref[prow, :]).astype(_BF16)
            upd = _pair_diag(_dot(kdt, v_ref[toks, 256 * pr:256 * (pr + 1)]))
            for hh in range(2):
                hd = 2 * pr + hh
                s_old = state_ref[hd]
                sb_ref[c, hd] = s_old.astype(_BF16)
                state_ref[hd] = s_old * gcb_ref[hd] + upd[hh]


def _proj_call(x, mod, g_mix, w_in_b, cos_t, sin_t, invc, dkbt, gcb, gain, pool_scale, *, tile):
    b, s, _ = x.shape
    nt = s // tile
    nc = s // CHUNK
    cpt = tile // CHUNK

    def tile_of_step(j):
        return nt - 1 - j

    def tok_spec(width):
        return pl.BlockSpec((None, tile, width), lambda bi, j: (bi, tile_of_step(j), 0))

    def kt_spec():
        return pl.BlockSpec((None, RET_QK, tile), lambda bi, j: (bi, 0, tile_of_step(j)))

    def edge_variant(bi, j):
        t = tile_of_step(j)
        return ((t == 0).astype(jnp.int32) + 2 * (t == nt - 1).astype(jnp.int32), 0, 0, 0)

    out_shape = (
        jax.ShapeDtypeStruct((b, s, RET_QK), _BF16),
        jax.ShapeDtypeStruct((b, RET_QK, s), _BF16),
        jax.ShapeDtypeStruct((b, s, RET_V), _BF16),
        jax.ShapeDtypeStruct((b, s, RET_V), _BF16),
        jax.ShapeDtypeStruct((b, s, POOL_WIDTH), _BF16),
        jax.ShapeDtypeStruct((b, s, D_MODEL), _BF16),
        jax.ShapeDtypeStruct((b, nc, N_HEADS, QK_DIM, V_DIM), _BF16),
    )
    out_specs = (
        tok_spec(RET_QK), kt_spec(), tok_spec(RET_V), tok_spec(RET_V),
        tok_spec(POOL_WIDTH), tok_spec(D_MODEL),
        pl.BlockSpec((None, cpt, N_HEADS, QK_DIM, V_DIM),
                     lambda bi, j: (bi, tile_of_step(j), 0, 0, 0)),
    )
    prev_spec, next_spec = _halo_specs(tile, s, tile_of_step)
    in_specs = [
        tok_spec(D_MODEL), prev_spec, next_spec,
        pl.BlockSpec((None, N_MOD, D_MODEL), lambda bi, j: (bi, 0, 0)),
        _const_spec((1, D_MODEL)),
        _const_spec((D_MODEL, IN_WIDTH)),
        pl.BlockSpec((tile, 128), lambda bi, j: (tile_of_step(j), 0)),
        pl.BlockSpec((tile, 128), lambda bi, j: (tile_of_step(j), 0)),
        pl.BlockSpec((None, N_POOL_GROUPS, tile, POOL_GROUP_IN), edge_variant),
        _const_spec((RET_QK, CHUNK)),
        _const_spec((N_HEADS, 1, V_DIM)),
        _const_spec((1, RET_V)),
        _const_spec((1, D_MODEL)),
    ]
    return pl.pallas_call(
        functools.partial(_proj_kernel, tile=tile),
        grid=(b, nt),
        in_specs=in_specs,
        out_specs=out_specs,
        out_shape=out_shape,
        scratch_shapes=[pltpu.VMEM((N_HEADS, QK_DIM, V_DIM), _F32),
                        pltpu.VMEM((RET_QK, tile), _F32)],
        compiler_params=pltpu.CompilerParams(
            dimension_semantics=("arbitrary", "arbitrary"),
            vmem_limit_bytes=VMEM_LIMIT_BYTES),
        name="proj",
    )(x, x, x, mod, g_mix, w_in_b, cos_t, sin_t, invc, dkbt, gcb, gain, pool_scale)


def _mix_kernel(mod_ref, q_ref, kt_ref, v_ref, gs_ref, pooled_ref, gbs_ref, sb_ref,
                dpair_ref, dqf_ref, dqb_ref, dkft_ref, gcf_ref, wpool_ref, wout_ref,
                o_ref, state_ref, mb_ref, merged_ref, *, tile):
    n_chunks = tile // CHUNK

    @pl.when(pl.program_id(1) == 0)
    def _():
        state_ref[...] = jnp.zeros_like(state_ref)

    for g in range(N_POOL_GROUPS):
        ocols = slice(POOL_GROUP_OUT * g, POOL_GROUP_OUT * (g + 1))
        bb = _dot(pooled_ref[:, POOL_GROUP_IN * g:POOL_GROUP_IN * (g + 1)], wpool_ref[g])
        mb_ref[:, ocols] = gbs_ref[:, ocols].astype(_F32) * bb

    zeros_v = jnp.zeros((CHUNK, V_DIM), _BF16)
    zeros_s = jnp.zeros((QK_DIM, V_DIM), _BF16)
    zeros_k = jnp.zeros((QK_DIM, CHUNK), _BF16)

    def pair_state(sa, sb):
        return jnp.concatenate([jnp.concatenate([sa, zeros_s], axis=1),
                                jnp.concatenate([zeros_s, sb], axis=1)], axis=0)

    for c in range(n_chunks):
        toks = slice(c * CHUNK, (c + 1) * CHUNK)
        for pr in range(N_PAIRS):
            cols = slice(128 * pr, 128 * (pr + 1))
            vcols = slice(256 * pr, 256 * (pr + 1))
            ha, hb = 2 * pr, 2 * pr + 1
            qp = q_ref[toks, cols]
            kt = kt_ref[cols, toks]
            vp = v_ref[toks, vcols]
            qf32 = qp.astype(_F32)
            kt_heads = jnp.concatenate([jnp.concatenate([kt[:QK_DIM], zeros_k], axis=0),
                                        jnp.concatenate([zeros_k, kt[QK_DIM:]], axis=0)], axis=1)
            scores = _dot(qp, kt_heads)
            pmat = (scores * dpair_ref[pr]).astype(_BF16)
            qf = (qf32 * dqf_ref[:, cols]).astype(_BF16)
            qb = (qf32 * dqb_ref[:, cols]).astype(_BF16)
            lhs = jnp.concatenate([pmat, qf, qb], axis=1)
            vbd = jnp.concatenate(
                [jnp.concatenate([vp[:, :V_DIM], zeros_v], axis=1),
                 jnp.concatenate([zeros_v, vp[:, V_DIM:]], axis=1)], axis=0)
            sf_a = state_ref[ha]
            sf_b = state_ref[hb]
            rhs = jnp.concatenate([vbd,
                                   pair_state(sf_a.astype(_BF16), sf_b.astype(_BF16)),
                                   pair_state(sb_ref[c, ha], sb_ref[c, hb])], axis=0)
            o = _dot(lhs, rhs)
            kdt = (kt.astype(_F32) * dkft_ref[cols, :]).astype(_BF16)
            upd_a, upd_b = _pair_diag(_dot(kdt, vp))
            state_ref[ha] = sf_a * gcf_ref[ha] + upd_a
            state_ref[hb] = sf_b * gcf_ref[hb] + upd_b
            normed = []
            for hh in range(2):
                oh = o[:, V_DIM * hh:V_DIM * (hh + 1)]
                d = oh - jnp.mean(oh, axis=-1, keepdims=True)
                var = jnp.mean(d * d, axis=-1, keepdims=True)
                normed.append(d * lax.rsqrt(var + EPS))
            merged = (gs_ref[toks, vcols].astype(_F32) * jnp.concatenate(normed, axis=1)
                      + mb_ref[toks, vcols])
            merged_ref[toks, vcols] = merged.astype(_BF16)

    o_ref[...] = mod_ref[2:3, :] * _dot(merged_ref[...], wout_ref[...])


def _mix_call(mod, q, kt, v, gs, pooled, gbs, sb, dpair, dqf, dqb, dkft, gcf,
              w_pool_b, w_out_b, *, tile):
    b, s, _ = q.shape
    nt = s // tile
    cpt = tile // CHUNK

    def tok_spec(width):
        return pl.BlockSpec((None, tile, width), lambda bi, j: (bi, j, 0))

    def kt_spec():
        return pl.BlockSpec((None, RET_QK, tile), lambda bi, j: (bi, 0, j))

    in_specs = [
        pl.BlockSpec((None, N_MOD, D_MODEL), lambda bi, j: (bi, 0, 0)),
        tok_spec(RET_QK), kt_spec(), tok_spec(RET_V), tok_spec(RET_V),
        tok_spec(POOL_WIDTH), tok_spec(D_MODEL),
        pl.BlockSpec((None, cpt, N_HEADS, QK_DIM, V_DIM), lambda bi, j: (bi, j, 0, 0, 0)),
        _const_spec((N_PAIRS, CHUNK, 2 * CHUNK)),
        _const_spec((CHUNK, RET_QK)), _const_spec((CHUNK, RET_QK)), _const_spec((RET_QK, CHUNK)),
        _const_spec((N_HEADS, 1, V_DIM)),
        _const_spec((N_POOL_GROUPS, POOL_GROUP_IN, POOL_GROUP_OUT)),
        _const_spec((D_MODEL, D_MODEL)),
    ]
    return pl.pallas_call(
        functools.partial(_mix_kernel, tile=tile),
        grid=(b, nt),
        in_specs=in_specs,
        out_specs=tok_spec(D_MODEL),
        out_shape=jax.ShapeDtypeStruct((b, s, D_MODEL), _F32),
        scratch_shapes=[pltpu.VMEM((N_HEADS, QK_DIM, V_DIM), _F32),
                        pltpu.VMEM((tile, D_MODEL), _F32),
                        pltpu.VMEM((tile, D_MODEL), _BF16)],
        compiler_params=pltpu.CompilerParams(
            dimension_semantics=("arbitrary", "arbitrary"),
            vmem_limit_bytes=VMEM_LIMIT_BYTES),
        name="mix",
    )(mod, q, kt, v, gs, pooled, gbs, sb, dpair, dqf, dqb, dkft, gcf, w_pool_b, w_out_b)


def _gelu_tanh(x):
    c = math.sqrt(2.0 / math.pi)
    half_x = 0.5 * x
    return half_x + half_x * jnp.tanh(x * (c + (c * 0.044715) * (x * x)))


def _ffn_kernel(x_ref, xprev_ref, xnext_ref, m_ref, mprev_ref, mnext_ref, mod_ref, g_ref, wup_ref,
                cw_ref, cb_ref, wdown_ref, gfin_ref, o_ref, x1_ref, act_head_ref, act_tail_ref, *, tile):
    t_idx = pl.program_id(1)
    ext = tile + 2 * HALO
    shift = mod_ref[3:4, :]
    gain = g_ref[...] * (1.0 + mod_ref[4:5, :])
    gate = mod_ref[5:6, :]

    x1 = x_ref[...] + m_ref[...]
    x1_ref[...] = x1
    h_prev = jnp.where(t_idx > 0,
                       _adaln_rmsnorm(xprev_ref[...] + mprev_ref[...], gain, shift), 0.0)
    h_next = jnp.where(t_idx < pl.num_programs(1) - 1,
                       _adaln_rmsnorm(xnext_ref[...] + mnext_ref[...], gain, shift), 0.0)
    hb = jnp.concatenate([_adaln_rmsnorm(x1, gain, shift), h_next, h_prev], axis=0).astype(_BF16)

    def conv(u, cols):
        out = (_shift_rows(u, 1) * cw_ref[0:1, cols] + u * cw_ref[1:2, cols]
               + _shift_rows(u, ext - 1) * cw_ref[2:3, cols])
        return out[:tile, :] + cb_ref[:, cols]

    def up(cols):
        return _dot_row_parts(hb, wup_ref[:, cols], FFN_ROW_PARTS)

    n_blocks = D_FF // FF_BLOCK
    n_head = n_blocks - FF_TAIL_BLOCKS
    for j in range(n_blocks):
        ca = slice(FF_BLOCK * j, FF_BLOCK * (j + 1))
        cl = slice(D_FF + FF_BLOCK * j, D_FF + FF_BLOCK * (j + 1))
        ua = up(ca)
        ul = up(cl)
        act = (_gelu_tanh(conv(ua, ca)) * conv(ul, cl)).astype(_BF16)
        if j < n_head:
            act_head_ref[:, ca] = act
        else:
            jt = j - n_head
            act_tail_ref[:, FF_BLOCK * jt:FF_BLOCK * (jt + 1)] = act

    k_head = n_head * FF_BLOCK
    ffn = (_dot(act_head_ref[...], wdown_ref[:k_head, :])
           + _dot(act_tail_ref[...], wdown_ref[k_head:, :]))
    y = x1_ref[...] + gate * ffn
    o_ref[...] = y * _rms_scale(y) * gfin_ref[...]


def _ffn_call(x, m, mod, g_ffn, w_up_b, conv_w, conv_b, w_down_b, g_final, *, tile):
    b, s, _ = x.shape
    nt = s // tile
    n_head = D_FF // FF_BLOCK - FF_TAIL_BLOCKS

    tok_spec = pl.BlockSpec((None, tile, D_MODEL), lambda bi, j: (bi, j, 0))
    prev_spec, next_spec = _halo_specs(tile, s, lambda j: j)
    in_specs = [
        tok_spec, prev_spec, next_spec, tok_spec, prev_spec, next_spec,
        pl.BlockSpec((None, N_MOD, D_MODEL), lambda bi, j: (bi, 0, 0)),
        _const_spec((1, D_MODEL)),
        _const_spec((D_MODEL, 2 * D_FF)),
        _const_spec((3, 2 * D_FF)),
        _const_spec((1, 2 * D_FF)),
        _const_spec((D_FF, D_MODEL)),
        _const_spec((1, D_MODEL)),
    ]
    return pl.pallas_call(
        functools.partial(_ffn_kernel, tile=tile),
        grid=(b, nt),
        in_specs=in_specs,
        out_specs=tok_spec,
        out_shape=jax.ShapeDtypeStruct((b, s, D_MODEL), _F32),
        scratch_shapes=[pltpu.VMEM((tile, D_MODEL), _F32),
                        pltpu.VMEM((tile, n_head * FF_BLOCK), _BF16),
                        pltpu.VMEM((tile, FF_TAIL_BLOCKS * FF_BLOCK), _BF16)],
        compiler_params=pltpu.CompilerParams(
            dimension_semantics=("arbitrary", "arbitrary"),
            vmem_limit_bytes=VMEM_LIMIT_BYTES),
        name="ffn",
    )(x, x, x, m, m, m, mod, g_ffn, w_up_b, conv_w, conv_b, w_down_b, g_final)


def _rope_tables(s):
    half = QK_DIM // 2
    inv_freq = ROPE_BASE ** (-jnp.arange(half, dtype=_F32) / half)
    ang = jnp.arange(s, dtype=_F32)[:, None] * inv_freq[None, :]
    cos = jnp.tile(jnp.cos(ang), (1, 4))
    sin = jnp.tile(jnp.sin(ang), (1, 4))
    first_half = (jnp.arange(128) % QK_DIM) < half
    return cos, jnp.where(first_half[None, :], -sin, sin)


def _pool_inv_counts(tile):
    row = jnp.arange(tile)
    variants = []
    for v in range(4):
        per_group = []
        for w in POOL_WINDOWS:
            lo = row - w // 2
            hi = row + w // 2
            if v & 1:
                lo = jnp.maximum(lo, 0)
            if v & 2:
                hi = jnp.minimum(hi, tile)
            inv = 1.0 / (hi - lo).astype(_F32)
            per_group.append(jnp.broadcast_to(inv[:, None], (tile, POOL_GROUP_IN)))
        variants.append(jnp.stack(per_group))
    return jnp.stack(variants)


def _decay_tables(ret_decay_f, ret_decay_b):
    lg_f = jax.nn.log_sigmoid(ret_decay_f.astype(_F32))
    lg_b = jax.nn.log_sigmoid(ret_decay_b.astype(_F32))
    idx = jnp.arange(CHUNK, dtype=_F32)
    diff = idx[:, None] - idx[None, :]
    causal = diff >= 0.0
    d_f = jnp.exp(jnp.where(causal, diff, 0.0)[None] * lg_f[:, None, None])
    d_b = jnp.exp(jnp.where(causal, 0.0, -diff)[None] * lg_b[:, None, None])
    d = jnp.where(causal[None], d_f, d_b)
    dpair = d.reshape(N_PAIRS, 2, CHUNK, CHUNK).transpose(0, 2, 1, 3).reshape(N_PAIRS, CHUNK, 2 * CHUNK)

    def per_lane(power, lg):
        return jnp.repeat(jnp.exp(power[:, None] * lg[None, :]), QK_DIM, axis=1)

    def chunk_decay(lg):
        return jnp.broadcast_to(jnp.exp(CHUNK * lg)[:, None, None], (N_HEADS, 1, V_DIM))

    return dict(dpair=dpair, dqf=per_lane(idx + 1.0, lg_f), dqb=per_lane(CHUNK - idx, lg_b),
                dkft=per_lane(CHUNK - 1.0 - idx, lg_f).T, dkbt=per_lane(idx, lg_b).T,
                gcf=chunk_decay(lg_f), gcb=chunk_decay(lg_b))


def _encoder(x, c, w, *, tile):
    b, s, _ = x.shape
    mod = _ada_call(c, w["w_ada"], w["b_ada"]).reshape(b, N_MOD, D_MODEL)
    cos_t, sin_t = _rope_tables(s)
    q, kt, v, gs, pooled, gbs, sb = _proj_call(
        x, mod, w["g_mix"], w["w_in"], cos_t, sin_t, _pool_inv_counts(tile), w["dkbt"], w["gcb"],
        w["gain"], w["pool_scale"], tile=tile)
    m = _mix_call(mod, q, kt, v, gs, pooled, gbs, sb, w["dpair"], w["dqf"], w["dqb"],
                  w["dkft"], w["gcf"], w["w_pool"], w["w_out"], tile=tile)
    return _ffn_call(x, m, mod, w["g_ffn"], w["w_up"], w["conv_w"], w["conv_b"], w["w_down"],
                     w["g_final"], tile=tile)


def _prepare_weights(w_ada, b_ada, g_mix, w_in, ret_decay_f, ret_decay_b, ret_gn_gain, w_pool_grp,
                     pool_scale, w_out, g_ffn, w_up, conv_w, conv_b, w_down, g_final):
    return dict(
        _decay_tables(ret_decay_f[0], ret_decay_b[0]),
        w_ada=w_ada[0].astype(_BF16), b_ada=b_ada[0],
        g_mix=g_mix[0].reshape(1, D_MODEL), w_in=w_in[0].astype(_BF16),
        gain=ret_gn_gain[0].reshape(1, RET_V), w_pool=w_pool_grp[0].astype(_BF16),
        pool_scale=pool_scale[0].reshape(1, D_MODEL), w_out=w_out[0].astype(_BF16),
        g_ffn=g_ffn[0].reshape(1, D_MODEL), w_up=w_up[0].astype(_BF16),
        conv_w=conv_w[0], conv_b=conv_b[0].reshape(1, 2 * D_FF),
        w_down=w_down[0].astype(_BF16), g_final=g_final.reshape(1, D_MODEL))


def kernel(x_prompt, x_sample, c_prompt, c_sample, w_ada, b_ada, g_mix, w_in, ret_decay_f,
           ret_decay_b, ret_gn_gain, w_pool_grp, pool_scale, w_out, g_ffn, w_up, conv_w, conv_b,
           w_down, g_final):
    assert w_ada.shape[0] == 1, "single-layer trunk"
    w = _prepare_weights(w_ada, b_ada, g_mix, w_in, ret_decay_f, ret_decay_b, ret_gn_gain,
                         w_pool_grp, pool_scale, w_out, g_ffn, w_up, conv_w, conv_b, w_down, g_final)
    y_prompt = _encoder(x_prompt, c_prompt, w, tile=min(512, x_prompt.shape[1]))
    y_sample = _encoder(x_sample, c_sample, w, tile=min(512, x_sample.shape[1]))
    return (y_prompt, y_sample)
```
